```python
import math
import jax, jax.numpy as jnp
from jax import lax
import numpy as np


D_MODEL = 1024
BATCH = 16
SEQ = 2048
DEPTH = 1
DEC_BATCH = 128
DEC_SEQ = 8
PAST_LEN = 8192
PAGE_SIZE = 128

FOX_HEADS = 8
FOX_HEAD_DIM = 64
FOX_WIDTH = FOX_HEADS * FOX_HEAD_DIM
Q_BLOCK = 128
FORGET_BIAS_LO = 2.0
FORGET_BIAS_HI = 10.0
GLA_HEADS = 4
GLA_KEY_DIM = D_MODEL // 2
GLA_VAL_DIM = D_MODEL
GLA_HK = GLA_KEY_DIM // GLA_HEADS
GLA_HV = GLA_VAL_DIM // GLA_HEADS
GLA_GATE_RANK = 16
GLA_GATE_TEMP = 16.0
GLA_CHUNK = 64
D_FF = -(-8 * D_MODEL // (3 * 256)) * 256
EPS = 1e-6
IN_SIZES = (FOX_WIDTH, FOX_WIDTH, FOX_WIDTH, FOX_HEADS,
            GLA_KEY_DIM, GLA_KEY_DIM, GLA_VAL_DIM, GLA_VAL_DIM, GLA_GATE_RANK,
            2 * D_MODEL)
IN_WIDTH = 3 * FOX_WIDTH + FOX_HEADS + 2 * GLA_KEY_DIM + 2 * GLA_VAL_DIM + GLA_GATE_RANK + 2 * D_MODEL

kernel_name = 'fox_gla_gated_parallel_decoder_step'


def rms_norm(x, gain):
    xf = x.astype(jnp.float32)
    y = xf * lax.rsqrt(jnp.mean(xf * xf, axis=-1, keepdims=True) + EPS)
    return (y * gain.astype(jnp.float32)).astype(x.dtype)


def modulate(h, shift, scale):
    return h * (1.0 + scale[:, None, :]) + shift[:, None, :]


def split_columns(z, sizes):
    offsets, acc = [], 0
    for s in sizes[:-1]:
        acc += s
        offsets.append(acc)
    return jnp.split(z, offsets, axis=-1)


def mixer_projections(h, w_in, b_forget, w_alpha_up, b_alpha):
    B, T, _ = h.shape
    z = h @ w_in
    q_a, k_a, v_a, f_a, q_b, k_b, v_b, r_b, a_low, g = split_columns(z, IN_SIZES)
    q_a = q_a.reshape(B, T, FOX_HEADS, FOX_HEAD_DIM)
    k_a = k_a.reshape(B, T, FOX_HEADS, FOX_HEAD_DIM)
    v_a = v_a.reshape(B, T, FOX_HEADS, FOX_HEAD_DIM)
    logf = jax.nn.log_sigmoid(f_a.astype(jnp.float32) + b_forget.astype(jnp.float32))
    q_b = q_b.reshape(B, T, GLA_HEADS, GLA_HK)
    k_b = k_b.reshape(B, T, GLA_HEADS, GLA_HK)
    v_b = v_b.reshape(B, T, GLA_HEADS, GLA_HV)
    a_logit = (a_low @ w_alpha_up + b_alpha).astype(jnp.float32)
    log_alpha = jax.nn.log_sigmoid(a_logit).reshape(B, T, GLA_HEADS, GLA_HK) / GLA_GATE_TEMP
    gate_a, gate_b = jnp.split(jax.nn.sigmoid(g), 2, axis=-1)
    return (q_a, k_a, v_a, logf), (q_b, k_b, v_b, log_alpha, r_b), (gate_a, gate_b)


def fox_prompt(q, k, v, logf):
    B, T, H, Dh = q.shape
    blk = math.gcd(T, Q_BLOCK)
    n_blk = T // blk
    scale = Dh ** -0.5
    cum = jnp.cumsum(logf, axis=1).transpose(0, 2, 1)
    q_blocks = q.reshape(B, n_blk, blk, H, Dh).swapaxes(0, 1)
    c_blocks = cum.reshape(B, H, n_blk, blk).transpose(2, 0, 1, 3)
    key_pos = jnp.arange(T)

    def one_block(args):
        i, q_i, c_i = args
        s = jnp.einsum('bqhd,bkhd->bhqk', q_i, k).astype(jnp.float32) * scale
        s = s + c_i[..., :, None] - cum[:, :, None, :]
        query_pos = i * blk + jnp.arange(blk)
        s = jnp.where(key_pos[None, :] <= query_pos[:, None], s, -jnp.inf)
        p = jax.nn.softmax(s, axis=-1).astype(v.dtype)
        return jnp.einsum('bhqk,bkhd->bqhd', p, v)

    out = lax.map(one_block, (jnp.arange(n_blk), q_blocks, c_blocks))
    return out.swapaxes(0, 1).reshape(B, T, H * Dh)


def fox_sample(q, k_new, v_new, logf_new, cache_k, cache_v, cache_logf, page_table, layer):
    DB, T, H, Dh = q.shape
    k_past = cache_k[layer][page_table].reshape(DB, -1, H, Dh)
    v_past = cache_v[layer][page_table].reshape(DB, -1, H, Dh)
    logf_past = cache_logf[layer][page_table].reshape(DB, -1, H).astype(jnp.float32)
    past = k_past.shape[1]
    scale = Dh ** -0.5
    suffix_past = (lax.cumsum(logf_past, axis=1, reverse=True) - logf_past).transpose(0, 2, 1)
    cq = jnp.cumsum(logf_new, axis=1).transpose(0, 2, 1)
    s_past = jnp.einsum('bthd,bshd->bhts', q, k_past).astype(jnp.float32) * scale
    s_past = s_past + cq[..., :, None] + suffix_past[..., None, :]
    s_new = jnp.einsum('bthd,bshd->bhts', q, k_new).astype(jnp.float32) * scale
    s_new = s_new + cq[..., :, None] - cq[..., None, :]
    causal = jnp.arange(T)[None, :] <= jnp.arange(T)[:, None]
    s_new = jnp.where(causal, s_new, -jnp.inf)
    p = jax.nn.softmax(jnp.concatenate([s_past, s_new], axis=-1), axis=-1).astype(v_new.dtype)
    y = (jnp.einsum('bhts,bshd->bthd', p[..., :past], v_past)
         + jnp.einsum('bhts,bshd->bthd', p[..., past:], v_new))
    return y.reshape(DB, T, H * Dh)


def gla_chunked(q, k, v, log_alpha, s0):
    B, T, H, DK = q.shape
    DV = v.shape[-1]
    C = math.gcd(T, GLA_CHUNK)
    n = T // C
    scale = DK ** -0.5

    def chunks(a):
        return a.reshape((B, n, C) + a.shape[2:]).swapaxes(0, 1)

    causal = jnp.arange(C)[None, :] <= jnp.arange(C)[:, None]

    def step(S, inp):
        qc, kc, vc, ac = inp
        qf = qc.astype(jnp.float32) * scale
        kf = kc.astype(jnp.float32)
        vf = vc.astype(jnp.float32)
        b = jnp.cumsum(ac, axis=1)
        o_inter = jnp.einsum('bthk,bhkv->bthv', qf * jnp.exp(b), S)
        diff = b[:, :, None] - b[:, None, :]
        decay = jnp.exp(jnp.where(causal[None, :, :, None, None], diff, -jnp.inf))
        A = jnp.einsum('bthk,bshk,btshk->bhts', qf, kf, decay)
        o_intra = jnp.einsum('bhts,bshv->bthv', A, vf)
        b_last = b[:, -1]
        k_dec = kf * jnp.exp(b_last[:, None] - b)
        S_new = jnp.exp(b_last)[..., None] * S + jnp.einsum('bshk,bshv->bhkv', k_dec, vf)
        return S_new, o_inter + o_intra

    S_final, o = lax.scan(step, s0.astype(jnp.float32),
                          (chunks(q), chunks(k), chunks(v), chunks(log_alpha)))
    return o.swapaxes(0, 1).reshape(B, T, H, DV), S_final


def decoder_layer(x, c, params, fox_attend, gla_s0):
    (w_ada, b_ada, norm_mix_pre, norm_mix_post, norm_ffn_pre, norm_ffn_post,
     w_in, b_forget, w_alpha_up, b_alpha, gla_norm, w_out_fox, w_out_gla, w_out,
     w_ffn_in, w_ffn_out) = params
    B, T, _ = x.shape
    ada = c @ w_ada + b_ada
    shift1, scale1, gate1, shift2, scale2, gate2 = jnp.split(ada, 6, axis=-1)
    h = modulate(rms_norm(x, norm_mix_pre), shift1, scale1)
    (q_a, k_a, v_a, logf), (q_b, k_b, v_b, log_alpha, r_b), (gate_a, gate_b) = \
        mixer_projections(h, w_in, b_forget, w_alpha_up, b_alpha)
    y_a = fox_attend(q_a, k_a, v_a, logf)
    o_b, s_new = gla_chunked(q_b, k_b, v_b, log_alpha, gla_s0)
    o_b = rms_norm(o_b.astype(h.dtype), gla_norm).reshape(B, T, GLA_VAL_DIM) * jax.nn.silu(r_b)
    merged = gate_a * (y_a @ w_out_fox) + gate_b * (o_b @ w_out_gla)
    x = x + gate1[:, None, :] * rms_norm(merged @ w_out, norm_mix_post)
    h = modulate(rms_norm(x, norm_ffn_pre), shift2, scale2)
    up, gt = jnp.split(h @ w_ffn_in, 2, axis=-1)
    f = (jax.nn.silu(gt) * up) @ w_ffn_out
    x = x + gate2[:, None, :] * rms_norm(f, norm_ffn_post)
    return x, (k_a, v_a, logf, s_new)


def setup_inputs(seed: int = 0) -> dict:
    key = jax.random.key(seed)
    ks = jax.random.split(key, 32)
    n_pages = PAST_LEN // PAGE_SIZE
    n_pool = (DEC_BATCH * n_pages * 5) // 4

    def normal(k, shape, scale):
        return jax.random.normal(k, shape, jnp.float32) * scale

    head_bias = jnp.linspace(FORGET_BIAS_LO, FORGET_BIAS_HI, FOX_HEADS, dtype=jnp.float32)
    perm = jax.random.permutation(ks[8], n_pool)
    page_table = perm[:DEC_BATCH * n_pages].reshape(DEC_BATCH, n_pages).astype(jnp.int32)
    return {
        'x_prompt': normal(ks[0], (BATCH, SEQ, D_MODEL), 1.0),
        'x_sample': normal(ks[1], (DEC_BATCH, DEC_SEQ, D_MODEL), 1.0),
        'c_prompt': normal(ks[2], (BATCH, D_MODEL), 1.0),
        'c_sample': normal(ks[3], (DEC_BATCH, D_MODEL), 1.0),
        'cache_k': normal(ks[4], (DEPTH, n_pool, PAGE_SIZE, FOX_HEADS, FOX_HEAD_DIM), 1.0),
        'cache_v': normal(ks[5], (DEPTH, n_pool, PAGE_SIZE, FOX_HEADS, FOX_HEAD_DIM), 1.0),
        'cache_logf': jax.nn.log_sigmoid(normal(ks[6], (DEPTH, n_pool, PAGE_SIZE, FOX_HEADS), 1.0) + head_bias),
        'state_gla': normal(ks[7], (DEPTH, DEC_BATCH, GLA_HEADS, GLA_HK, GLA_HV), 0.5),
        'page_table': page_table,
        'w_ada': normal(ks[9], (DEPTH, D_MODEL, 6 * D_MODEL), 0.5 * D_MODEL ** -0.5),
        'b_ada': normal(ks[10], (DEPTH, 6 * D_MODEL), 0.01),
        'norm_mix_pre': 1.0 + normal(ks[11], (DEPTH, D_MODEL), 0.05),
        'norm_mix_post': 1.0 + normal(ks[12], (DEPTH, D_MODEL), 0.05),
        'norm_ffn_pre': 1.0 + normal(ks[13], (DEPTH, D_MODEL), 0.05),
        'norm_ffn_post': 1.0 + normal(ks[14], (DEPTH, D_MODEL), 0.05),
        'w_in': normal(ks[15], (DEPTH, D_MODEL, IN_WIDTH), D_MODEL ** -0.5),
        'b_forget': head_bias + normal(ks[16], (DEPTH, FOX_HEADS), 0.1),
        'w_alpha_up': normal(ks[17], (DEPTH, GLA_GATE_RANK, GLA_KEY_DIM), GLA_GATE_RANK ** -0.5),
        'b_alpha': normal(ks[18], (DEPTH, GLA_KEY_DIM), 0.01),
        'gla_norm': 1.0 + normal(ks[19], (DEPTH, GLA_HV), 0.05),
        'w_out_fox': normal(ks[20], (DEPTH, FOX_WIDTH, D_MODEL), FOX_WIDTH ** -0.5),
        'w_out_gla': normal(ks[21], (DEPTH, GLA_VAL_DIM, D_MODEL), GLA_VAL_DIM ** -0.5),
        'w_out': normal(ks[22], (DEPTH, D_MODEL, D_MODEL), D_MODEL ** -0.5),
        'w_ffn_in': normal(ks[23], (DEPTH, D_MODEL, 2 * D_FF), D_MODEL ** -0.5),
        'w_ffn_out': normal(ks[24], (DEPTH, D_FF, D_MODEL), D_FF ** -0.5),
    }


def reference(x_prompt, x_sample, c_prompt, c_sample, cache_k, cache_v, cache_logf, state_gla,
              page_table, w_ada, b_ada, norm_mix_pre, norm_mix_post, norm_ffn_pre, norm_ffn_post,
              w_in, b_forget, w_alpha_up, b_alpha, gla_norm, w_out_fox, w_out_gla, w_out,
              w_ffn_in, w_ffn_out):
    yp, ys = x_prompt, x_sample
    kp_l, vp_l, fp_l, sp_l = [], [], [], []
    ks_l, vs_l, fs_l, ss_l = [], [], [], []
    for layer in range(DEPTH):
        params = (w_ada[layer], b_ada[layer], norm_mix_pre[layer], norm_mix_post[layer],
                  norm_ffn_pre[layer], norm_ffn_post[layer], w_in[layer], b_forget[layer],
                  w_alpha_up[layer], b_alpha[layer], gla_norm[layer], w_out_fox[layer],
                  w_out_gla[layer], w_out[layer], w_ffn_in[layer], w_ffn_out[layer])
        s0_prompt = jnp.zeros((x_prompt.shape[0], GLA_HEADS, GLA_HK, GLA_HV), jnp.float32)
        yp, (kp, vp, fp, sp) = decoder_layer(yp, c_prompt, params, fox_prompt, s0_prompt)
        attend = lambda q, k, v, lf, layer=layer: fox_sample(q, k, v, lf, cache_k, cache_v,
                                                              cache_logf, page_table, layer)
        ys, (k_s, v_s, f_s, s_s) = decoder_layer(ys, c_sample, params, attend, state_gla[layer])
        kp_l.append(kp); vp_l.append(vp); fp_l.append(fp); sp_l.append(sp)
        ks_l.append(k_s); vs_l.append(v_s); fs_l.append(f_s); ss_l.append(s_s)
    k_prompt = jnp.stack(kp_l)
    v_prompt = jnp.stack(vp_l)
    logf_prompt = jnp.stack(fp_l)
    gla_state_prompt = jnp.stack(sp_l)
    k_sample = jnp.stack(ks_l)
    v_sample = jnp.stack(vs_l)
    logf_sample = jnp.stack(fs_l)
    gla_state_sample = jnp.stack(ss_l)
    return (yp, ys, k_prompt, v_prompt, logf_prompt, gla_state_prompt,
            k_sample, v_sample, logf_sample, gla_state_sample)
```

```python
import functools
import math

import jax
import jax.numpy as jnp
from jax import lax
from jax.experimental import pallas as pl
from jax.experimental.pallas import tpu as pltpu

F32 = jnp.float32
BF16 = jnp.bfloat16

FOX_HEADS = 8
FOX_HEAD_DIM = 64
FOX_WIDTH = FOX_HEADS * FOX_HEAD_DIM
GLA_HEADS = 4
GLA_HK = 128
GLA_HV = 256
GLA_KEY_DIM = GLA_HEADS * GLA_HK
GLA_VAL_DIM = GLA_HEADS * GLA_HV
GLA_GATE_RANK = 16
GLA_GATE_TEMP = 16.0
EPS = 1e-6

LANES = 128
VMEM_LIMIT_BYTES = 56 * 1024 * 1024
TOKEN_TILE = 256
GLA_DIAG_BLOCK = 16
GLA_PROMPT_CHUNK = 128
FOX_Q_TILE = 256
FOX_K_TILE = 256
DECODE_PAGES_PER_STEP = 4
NEG_INF = float("-inf")


def _params(*sem):
    return pltpu.CompilerParams(dimension_semantics=sem, vmem_limit_bytes=VMEM_LIMIT_BYTES)


def _dot(a, b):
    return jnp.dot(a, b, preferred_element_type=F32)


def _dot_nt(a, b):
    return lax.dot_general(a, b, (((1,), (1,)), ((), ())), preferred_element_type=F32)


def _split3(x):
    hi = x.astype(BF16)
    r1 = x - hi.astype(F32)
    mid = r1.astype(BF16)
    lo = (r1 - mid.astype(F32)).astype(BF16)
    return hi, mid, lo


def _dot_exact_lhs01(l01, x):
    hi, mid, lo = _split3(x)
    return _dot(l01, hi) + _dot(l01, mid) + _dot(l01, lo)


def _log_sigmoid(x):
    return jnp.minimum(x, 0.0) - jnp.log1p(jnp.exp(-jnp.abs(x)))


def _sigmoid(x):
    return 1.0 / (1.0 + jnp.exp(-x))


def _rms(x):
    return x * lax.rsqrt(jnp.mean(x * x, axis=-1, keepdims=True) + EPS)


def _ada_kernel(c_ref, w_ref, b_ref, o_ref):
    o_ref[...] = _dot(c_ref[...].astype(BF16), w_ref[...].astype(BF16)) + b_ref[...]


def _ada_proj(c, w, b):
    n, d = c.shape
    cols = w.shape[1]
    bn = 1024
    return pl.pallas_call(
        _ada_kernel,
        out_shape=jax.ShapeDtypeStruct((n, cols), F32),
        grid=(cols // bn,),
        in_specs=[pl.BlockSpec((n, d), lambda j: (0, 0)),
                  pl.BlockSpec((d, bn), lambda j: (0, j)),
                  pl.BlockSpec((1, bn), lambda j: (0, j))],
        out_specs=pl.BlockSpec((n, bn), lambda j: (0, j)),
        compiler_params=_params("arbitrary"),
        name="ada_proj",
    )(c, w, b)


def _in_proj_kernel(x_ref, ada_ref, g_ref, wa_ref, ws_ref, wb_ref, wg_ref, bf_ref, wup_ref,
                    bal_ref, lmat_ref,
                    qa_ref, kaf_ref, kab_ref, vaf_ref, vab_ref, logf_ref, cum_ref,
                    qb_ref, kb_ref, vb_ref, rb_ref, la_ref, gate_ref, carry_ref,
                    *, carry_rows):
    tb, tt, d = x_ref.shape
    m = tb * tt
    x = x_ref[...]
    y = _rms(x) * g_ref[...]
    shift = ada_ref[:, :, 0:d]
    scale = ada_ref[:, :, d:2 * d]
    h = (y * (1.0 + scale) + shift).reshape(m, d).astype(BF16)

    za = _dot(h, wa_ref[...])
    w = FOX_WIDTH
    qa_ref[...] = (za[:, 0:w] * (FOX_HEAD_DIM ** -0.5)).astype(qa_ref.dtype)
    ka = za[:, w:2 * w]
    va = za[:, 2 * w:3 * w]
    kaf_ref[...] = ka
    kab_ref[...] = ka.astype(BF16)
    vaf_ref[...] = va
    vab_ref[...] = va.astype(BF16)

    zs = _dot(h, ws_ref[...])
    logf = _log_sigmoid(zs[:, 0:LANES] + bf_ref[...])
    logf_ref[...] = logf[:, 0:FOX_HEADS]
    cum = _dot_exact_lhs01(lmat_ref[...], logf)
    if carry_rows:
        @pl.when(pl.program_id(1) == 0)
        def _():
            carry_ref[...] = jnp.zeros_like(carry_ref)
        cum = cum + carry_ref[0:1, :]
        carry_ref[0:1, :] = cum[m - 1:m, :]
    cum_ref[...] = cum[:, 0:FOX_HEADS]

    a_low = zs[:, LANES:2 * LANES]
    a_hi = a_low.astype(BF16)
    a_mid = (a_low - a_hi.astype(F32)).astype(BF16)
    wup = wup_ref[...]
    w_hi = wup.astype(BF16)
    w_mid = (wup - w_hi.astype(F32)).astype(BF16)
    a_logit = _dot(a_hi, w_hi) + _dot(a_hi, w_mid) + _dot(a_mid, w_hi) + bal_ref[...]
    la_ref[...] = _log_sigmoid(a_logit) * (1.0 / GLA_GATE_TEMP)

    zb = _dot(h, wb_ref[...])
    kd, vd = GLA_KEY_DIM, GLA_VAL_DIM
    qb_ref[...] = zb[:, 0:kd].astype(qb_ref.dtype)
    kb_ref[...] = zb[:, kd:2 * kd].astype(kb_ref.dtype)
    vb_ref[...] = zb[:, 2 * kd:2 * kd + vd].astype(vb_ref.dtype)
    rb_ref[...] = zb[:, 2 * kd + vd:2 * kd + 2 * vd].astype(rb_ref.dtype)

    gate_ref[...] = _sigmoid(_dot(h, wg_ref[...])).astype(gate_ref.dtype)


def _in_proj(x, ada3, gain, wts, *, tb, tt, act_dtype):
    bsz, t, d = x.shape
    m = tb * tt
    n_t = t // tt
    n_tok = bsz * t
    carry_rows = n_t > 1
    assert (tb == 1) or (tt == t)
    row = lax.broadcasted_iota(jnp.int32, (m, m), 0)
    col = lax.broadcasted_iota(jnp.int32, (m, m), 1)
    lmat = ((col <= row) & (row // tt == col // tt)).astype(BF16)

    def full(a):
        return pl.BlockSpec(a.shape, lambda i, j: (0,) * a.ndim)

    def tok(cols):
        return pl.BlockSpec((m, cols), lambda i, j: (i * n_t + j, 0))

    out_cols = [(FOX_WIDTH, act_dtype), (FOX_WIDTH, F32), (FOX_WIDTH, BF16), (FOX_WIDTH, F32),
                (FOX_WIDTH, BF16), (FOX_HEADS, F32), (FOX_HEADS, F32),
                (GLA_KEY_DIM, act_dtype), (GLA_KEY_DIM, act_dtype), (GLA_VAL_DIM, act_dtype),
                (GLA_VAL_DIM, act_dtype), (GLA_KEY_DIM, F32), (2 * d, BF16)]
    consts = (gain, wts["wa"], wts["ws"], wts["wb"], wts["wg"], wts["bf"], wts["wup"], wts["bal"], lmat)
    return pl.pallas_call(
        functools.partial(_in_proj_kernel, carry_rows=carry_rows),
        out_shape=[jax.ShapeDtypeStruct((n_tok, c), dt) for c, dt in out_cols],
        grid=(bsz // tb, n_t),
        in_specs=[pl.BlockSpec((tb, tt, d), lambda i, j: (i, j, 0)),
                  pl.BlockSpec((tb, 1, ada3.shape[2]), lambda i, j: (i, 0, 0))]
                 + [full(a) for a in consts],
        out_specs=[tok(c) for c, _ in out_cols],
        scratch_shapes=[pltpu.VMEM((8, LANES), F32)],
        compiler_params=_params("arbitrary", "arbitrary"),
        name="in_proj",
    )(x, ada3, *consts)


def _fox_prompt_kernel(q_ref, k_ref, v_ref, cq_ref, ck_ref, o_ref, *, tq, tk):
    qi = pl.program_id(2)
    q = q_ref[...]
    lane = lax.broadcasted_iota(jnp.int32, (1, LANES), 1)
    lo_half = lane < FOX_HEAD_DIM
    q_heads = (jnp.where(lo_half, q, jnp.zeros_like(q)), jnp.where(lo_half, jnp.zeros_like(q), q))
    cq = cq_ref[0, 0]
    cq_heads = (cq[:, 0:1], cq[:, 1:2])
    row = lax.broadcasted_iota(jnp.int32, (tq, tk), 0)
    col = lax.broadcasted_iota(jnp.int32, (tq, tk), 1)

    def block(ki, carry, masked):
        k0 = pl.multiple_of(ki * tk, tk)
        k = k_ref[pl.ds(k0, tk), :]
        v = v_ref[pl.ds(k0, tk), :]
        ck = ck_ref[0, 0, :, pl.ds(k0, tk)]
        out = []
        for g in range(2):
            m_prev, l_prev, acc_prev = carry[g]
            s = _dot_nt(q_heads[g], k) + cq_heads[g] - ck[g:g + 1, :]
            if masked:
                s = jnp.where(col <= row, s, NEG_INF)
            m_new = jnp.maximum(m_prev, jnp.max(s, axis=1, keepdims=True))
            alpha = jnp.exp(m_prev - m_new)
            p = jnp.exp(s - m_new)
            l_new = alpha * l_prev + jnp.sum(p, axis=1, keepdims=True)
            acc_new = alpha * acc_prev + _dot(p.astype(BF16), v)
            out.append((m_new, l_new, acc_new))
        return tuple(out)

    init = tuple((jnp.full((tq, 1), NEG_INF, F32), jnp.zeros((tq, 1), F32),
                  jnp.zeros((tq, LANES), F32)) for _ in range(2))
    carry = lax.fori_loop(0, qi, lambda ki, c: block(ki, c, False), init)
    carry = block(qi, carry, True)
    (m0, l0, a0), (m1, l1, a1) = carry
    o_ref[...] = jnp.where(lo_half, a0 / l0, a1 / l1).astype(o_ref.dtype)


def _fox_prompt(q, k, v, cum, bsz, t):
    tq = min(FOX_Q_TILE, t)
    tk = tq
    n_q = t // tq
    pairs = FOX_HEADS // 2
    cum4 = cum.reshape(bsz, t, pairs, 2)
    cq = cum4.transpose(0, 2, 1, 3)
    ck = cum4.transpose(0, 2, 3, 1)
    return pl.pallas_call(
        functools.partial(_fox_prompt_kernel, tq=tq, tk=tk),
        out_shape=jax.ShapeDtypeStruct((bsz * t, FOX_WIDTH), BF16),
        grid=(bsz, pairs, n_q),
        in_specs=[pl.BlockSpec((tq, LANES), lambda b, hp, i: (b * n_q + i, hp)),
                  pl.BlockSpec((t, LANES), lambda b, hp, i: (b, hp)),
                  pl.BlockSpec((t, LANES), lambda b, hp, i: (b, hp)),
                  pl.BlockSpec((1, 1, tq, 2), lambda b, hp, i: (b, hp, i, 0)),
                  pl.BlockSpec((1, 1, 2, t), lambda b, hp, i: (b, hp, 0, 0))],
        out_specs=pl.BlockSpec((tq, LANES), lambda b, hp, i: (b * n_q + i, hp)),
        compiler_params=_params("arbitrary", "arbitrary", "arbitrary"),
        name="fox_prompt",
    )(q, k, v, cq, ck)


def _lane_cumsum(x):
    lane = lax.broadcasted_iota(jnp.int32, x.shape, 1)
    sh = 1
    while sh < LANES:
        x = x + jnp.where(lane >= sh, pltpu.roll(x, sh, 1), 0.0)
        sh *= 2
    return x


def _fox_sample_kernel(pt_ref, q_ref, cq_ref, *refs, pages, n_tok):
    k_refs = refs[0:pages]
    v_refs = refs[pages:2 * pages]
    f_refs = refs[2 * pages:3 * pages]
    kn_ref, vn_ref, fn_ref, o_ref, m_ref, l_ref, acc_ref, carry_ref = refs[3 * pages:]
    j = pl.program_id(1)
    rows = FOX_HEADS * n_tok

    @pl.when(j == 0)
    def _():
        m_ref[...] = jnp.full_like(m_ref, NEG_INF)
        l_ref[...] = jnp.zeros_like(l_ref)
        acc_ref[...] = jnp.zeros_like(acc_ref)
        carry_ref[...] = jnp.zeros_like(carry_ref)

    q = q_ref[0]
    cq = cq_ref[0]

    def attend(k_list, v_list, bias_list, mask):
        s_heads = []
        for h in range(FOX_HEADS):
            parts = []
            for kp, bias in zip(k_list, bias_list):
                kh = kp(h).astype(BF16)
                parts.append(_dot_nt(q[h], kh) + bias[h:h + 1, :])
            s_heads.append(jnp.concatenate(parts, axis=1) if len(parts) > 1 else parts[0])
        s = jnp.concatenate(s_heads, axis=0) + cq
        if mask is not None:
            s = jnp.where(mask, s, NEG_INF)
        m_prev = m_ref[...]
        m_new = jnp.maximum(m_prev, jnp.max(s, axis=1, keepdims=True))
        alpha = jnp.exp(m_prev - m_new)
        p = jnp.exp(s - m_new)
        l_ref[...] = alpha * l_ref[...] + jnp.sum(p, axis=1, keepdims=True)
        m_ref[...] = m_new
        pv_heads = []
        for h in range(FOX_HEADS):
            acc_h = None
            for g, vp in enumerate(v_list):
                ph = p[h * n_tok:(h + 1) * n_tok, g * LANES:(g + 1) * LANES].astype(BF16)
                vh = vp(h).astype(BF16)
                d = _dot(ph, vh)
                acc_h = d if acc_h is None else acc_h + d
            pv_heads.append(acc_h)
        acc_ref[...] = alpha * acc_ref[...] + jnp.concatenate(pv_heads, axis=0)

    k_list = [functools.partial(lambda r, h: r[0, 0, :, h, :], r) for r in k_refs]
    v_list = [functools.partial(lambda r, h: r[0, 0, :, h, :], r) for r in v_refs]
    bias_list = []
    carry = carry_ref[...]
    for r in f_refs:
        lf = r[0]
        bias_list.append(-(_lane_cumsum(lf) + carry))
        carry = carry + jnp.sum(lf, axis=1, keepdims=True)
    carry_ref[...] = carry
    attend(k_list, v_list, bias_list, None)

    @pl.when(j == pl.num_programs(1) - 1)
    def _():
        total = carry_ref[...]
        cn = _lane_cumsum(fn_ref[0])
        row = lax.broadcasted_iota(jnp.int32, (rows, LANES), 0)
        lane = lax.broadcasted_iota(jnp.int32, (rows, LANES), 1)
        mask = (lane <= row % n_tok) & (lane < n_tok)
        attend([lambda h: kn_ref[0, :, h, :]], [lambda h: vn_ref[0, :, h, :]], [-(cn + total)], mask)
        o_ref[0] = acc_ref[...] / l_ref[...]


def _fox_sample(q, cq, k_new, v_new, logf_new, cache_k, cache_v, cache_logf_t, page_table):
    db, n_tok, _ = q.shape
    n_pages = page_table.shape[1]
    page = cache_k.shape[2]
    pages = math.gcd(DECODE_PAGES_PER_STEP, n_pages)
    rows = FOX_HEADS * n_tok
    h, dh = FOX_HEADS, FOX_HEAD_DIM
    q4 = q.reshape(db, n_tok, h, dh).transpose(0, 2, 1, 3).astype(BF16)
    cq_col = cq.transpose(0, 2, 1).reshape(db, rows, 1)

    def pad_page(a):
        return jnp.pad(a.reshape(db, n_tok, h, dh), ((0, 0), (0, page - n_tok), (0, 0), (0, 0)))

    kn = pad_page(k_new)
    vn = pad_page(v_new)
    fn = jnp.pad(logf_new.transpose(0, 2, 1), ((0, 0), (0, 0), (0, page - n_tok)))

    def paged5(g):
        return pl.BlockSpec((1, 1, page, h, dh),
                            lambda b, j, pt: (0, pt[b * n_pages + j * pages + g], 0, 0, 0))

    def paged3(g):
        return pl.BlockSpec((1, h, page), lambda b, j, pt: (pt[b * n_pages + j * pages + g], 0, 0))

    grid_spec = pltpu.PrefetchScalarGridSpec(
        num_scalar_prefetch=1,
        grid=(db, n_pages // pages),
        in_specs=[pl.BlockSpec((1, h, n_tok, dh), lambda b, j, pt: (b, 0, 0, 0)),
                  pl.BlockSpec((1, rows, 1), lambda b, j, pt: (b, 0, 0))]
                 + [paged5(g) for g in range(pages)]
                 + [paged5(g) for g in range(pages)]
                 + [paged3(g) for g in range(pages)]
                 + [pl.BlockSpec((1, page, h, dh), lambda b, j, pt: (b, 0, 0, 0)),
                    pl.BlockSpec((1, page, h, dh), lambda b, j, pt: (b, 0, 0, 0)),
                    pl.BlockSpec((1, h, page), lambda b, j, pt: (b, 0, 0))],
        out_specs=pl.BlockSpec((1, rows, dh), lambda b, j, pt: (b, 0, 0)),
        scratch_shapes=[pltpu.VMEM((rows, 1), F32), pltpu.VMEM((rows, 1), F32),
                        pltpu.VMEM((rows, dh), F32), pltpu.VMEM((h, 1), F32)],
    )
    out = pl.pallas_call(
        functools.partial(_fox_sample_kernel, pages=pages, n_tok=n_tok),
        out_shape=jax.ShapeDtypeStruct((db, rows, dh), F32),
        grid_spec=grid_spec,
        compiler_params=_params("arbitrary", "arbitrary"),
        name="fox_sample",
    )(page_table.reshape(-1), q4, cq_col,
      *([cache_k] * pages), *([cache_v] * pages), *([cache_logf_t] * pages), kn, vn, fn)
    return out.reshape(db, h, n_tok, dh).transpose(0, 2, 1, 3).reshape(db * n_tok, FOX_WIDTH)


def _gla_spans(lo, hi, blk):
    if hi - lo <= blk:
        return []
    mid = (lo + hi) // 2
    return [(lo, mid, hi)] + _gla_spans(lo, mid, blk) + _gla_spans(mid, hi, blk)


def _gla_kernel(q_ref, k_ref, v_ref, la_ref, r_ref, gain_ref, *rest, chunk, blk, has_s0):
    if has_s0:
        s0_ref, o_ref, sout_ref, st_ref = rest
    else:
        o_ref, sout_ref, st_ref = rest
    c = pl.program_id(1)
    cp = max(chunk, LANES)

    @pl.when(c == 0)
    def _():
        for hd in range(GLA_HEADS):
            if has_s0:
                st_ref[hd] = s0_ref[0, hd].T
            else:
                st_ref[hd] = jnp.zeros((GLA_HV, GLA_HK), F32)

    la = la_ref[...]
    if chunk <= 8:
        b = jnp.zeros_like(la)
        rowi = lax.broadcasted_iota(jnp.int32, (chunk, 1), 0)
        for s in range(chunk):
            b = b + jnp.where(rowi >= s, la[s:s + 1, :], 0.0)
    else:
        ri = lax.broadcasted_iota(jnp.int32, (chunk, chunk), 0)
        ci = lax.broadcasted_iota(jnp.int32, (chunk, chunk), 1)
        b = _dot_exact_lhs01((ci <= ri).astype(BF16), la)

    spans = _gla_spans(0, chunk, blk)
    scale = GLA_HK ** -0.5
    ti = lax.broadcasted_iota(jnp.int32, (blk, blk, 1), 0)
    si = lax.broadcasted_iota(jnp.int32, (blk, blk, 1), 1)
    tri = si <= ti
    gain = gain_ref[...]

    for hd in range(GLA_HEADS):
        ks = slice(hd * GLA_HK, (hd + 1) * GLA_HK)
        vs = slice(hd * GLA_HV, (hd + 1) * GLA_HV)
        bh = b[:, ks]
        qh = q_ref[:, ks].astype(F32) * scale
        kh = k_ref[:, ks].astype(F32)
        vh = v_ref[:, vs].astype(F32)
        b_last = bh[chunk - 1:chunk, :]
        st = st_ref[hd]

        q0 = (qh * jnp.exp(bh)).astype(BF16)
        o = _dot_nt(q0, st.astype(BF16))

        if spans:
            q_parts, k_parts = [], []
            for lo, mid, hi in spans:
                bref = bh[mid - 1:mid, :]
                qg = qh[mid:hi] * jnp.exp(bh[mid:hi] - bref)
                kg = kh[lo:mid] * jnp.exp(bref - bh[lo:mid])
                qz = [jnp.zeros((mid, GLA_HK), F32), qg]
                if chunk > hi:
                    qz.append(jnp.zeros((chunk - hi, GLA_HK), F32))
                kz = ([jnp.zeros((lo, GLA_HK), F32)] if lo > 0 else []) + [kg, jnp.zeros((chunk - mid, GLA_HK), F32)]
                q_parts.append(jnp.concatenate(qz, axis=0))
                k_parts.append(jnp.concatenate(kz, axis=0))
            q_hat = jnp.concatenate(q_parts, axis=1).astype(BF16)
            k_hat = jnp.concatenate(k_parts, axis=1).astype(BF16)
            a_mat = _dot_nt(q_hat, k_hat)
        else:
            a_mat = None
        diag_rows = []
        for i in range(chunk // blk):
            r0 = i * blk
            qi, ki, bi = qh[r0:r0 + blk], kh[r0:r0 + blk], bh[r0:r0 + blk]
            dec = jnp.where(tri, jnp.exp(bi[:, None, :] - bi[None, :, :]), 0.0)
            a_ii = jnp.sum(qi[:, None, :] * ki[None, :, :] * dec, axis=-1)
            if chunk == blk:
                diag_rows.append(a_ii)
            else:
                pieces = ([jnp.zeros((blk, r0), F32)] if r0 > 0 else []) + [a_ii]
                if chunk - r0 - blk > 0:
                    pieces.append(jnp.zeros((blk, chunk - r0 - blk), F32))
                diag_rows.append(jnp.concatenate(pieces, axis=1))
        a_diag = jnp.concatenate(diag_rows, axis=0) if len(diag_rows) > 1 else diag_rows[0]
        a_mat = a_diag if a_mat is None else a_mat + a_diag

        if chunk <= 8:
            for s in range(chunk):
                o = o + a_mat[:, s:s + 1] * vh[s:s + 1, :]
        else:
            o = o + _dot(a_mat.astype(BF16), vh.astype(BF16))

        k0 = kh * jnp.exp(b_last - bh)
        v_t = vh
        if cp > chunk:
            k0 = jnp.concatenate([k0, jnp.zeros((cp - chunk, GLA_HK), F32)], axis=0)
            v_t = jnp.concatenate([vh, jnp.zeros((cp - chunk, GLA_HV), F32)], axis=0)
        st_new = st * jnp.exp(b_last) + _dot(v_t.T.astype(BF16), k0.astype(BF16))
        st_ref[hd] = st_new

        r = r_ref[:, vs].astype(F32)
        o_ref[:, vs] = (_rms(o) * gain * (r * _sigmoid(r))).astype(o_ref.dtype)

    @pl.when(c == pl.num_programs(1) - 1)
    def _():
        for hd in range(GLA_HEADS):
            sout_ref[0, hd] = st_ref[hd].T


def _gla(q, k, v, la, r, gain, s0, bsz, t, chunk):
    n_c = t // chunk
    blk = min(GLA_DIAG_BLOCK, chunk)
    has_s0 = s0 is not None

    def tok(cols):
        return pl.BlockSpec((chunk, cols), lambda b, c: (b * n_c + c, 0))

    state_spec = pl.BlockSpec((1, GLA_HEADS, GLA_HK, GLA_HV), lambda b, c: (b, 0, 0, 0))
    in_specs = [tok(GLA_KEY_DIM), tok(GLA_KEY_DIM), tok(GLA_VAL_DIM), tok(GLA_KEY_DIM), tok(GLA_VAL_DIM),
                pl.BlockSpec((1, GLA_HV), lambda b, c: (0, 0))]
    args = [q, k, v, la, r, gain]
    if has_s0:
        in_specs.append(state_spec)
        args.append(s0)
    return pl.pallas_call(
        functools.partial(_gla_kernel, chunk=chunk, blk=blk, has_s0=has_s0),
        out_shape=[jax.ShapeDtypeStruct((bsz * t, GLA_VAL_DIM), BF16),
                   jax.ShapeDtypeStruct((bsz, GLA_HEADS, GLA_HK, GLA_HV), F32)],
        grid=(bsz, n_c),
        in_specs=in_specs,
        out_specs=[tok(GLA_VAL_DIM), state_spec],
        scratch_shapes=[pltpu.VMEM((GLA_HEADS, GLA_HV, GLA_HK), F32)],
        compiler_params=_params("arbitrary", "arbitrary"),
        name="gla",
    )(*args)


def _mix_out_kernel(x_ref, ada_ref, ya_ref, ob_ref, gate_ref, wfox_ref, wgla_ref, wout_ref, g_ref, o_ref):
    tb, tt, d = x_ref.shape
    m = tb * tt
    gates = gate_ref[...].astype(F32)
    merged = (gates[:, 0:d] * _dot(ya_ref[...], wfox_ref[...])
              + gates[:, d:2 * d] * _dot(ob_ref[...], wgla_ref[...]))
    y = _dot(merged.astype(BF16), wout_ref[...])
    y = (_rms(y) * g_ref[...]).reshape(tb, tt, d)
    o_ref[...] = x_ref[...] + ada_ref[:, :, 2 * d:3 * d] * y


def _ffn_kernel(x_ref, ada_ref, gpre_ref, win_ref, wout_ref, gpost_ref, o_ref, *, d_ff, fc):
    tb, tt, d = x_ref.shape
    m = tb * tt
    x = x_ref[...]
    y = _rms(x) * gpre_ref[...]
    h = (y * (1.0 + ada_ref[:, :, 4 * d:5 * d]) + ada_ref[:, :, 3 * d:4 * d]).reshape(m, d).astype(BF16)
    acc = jnp.zeros((m, d), F32)
    for c0 in range(0, d_ff, fc):
        up = _dot(h, win_ref[:, c0:c0 + fc])
        gt = _dot(h, win_ref[:, d_ff + c0:d_ff + c0 + fc])
        acc = acc + _dot((gt * _sigmoid(gt) * up).astype(BF16), wout_ref[c0:c0 + fc, :])
    f = (_rms(acc) * gpost_ref[...]).reshape(tb, tt, d)
    o_ref[...] = x + ada_ref[:, :, 5 * d:6 * d] * f


def _tail(x, ada3, y_a, o_b, gates, wts, *, tb, tt):
    bsz, t, d = x.shape
    m = tb * tt
    n_t = t // tt

    def full(a):
        return pl.BlockSpec(a.shape, lambda i, j: (0,) * a.ndim)

    def tok(cols):
        return pl.BlockSpec((m, cols), lambda i, j: (i * n_t + j, 0))

    x_spec = pl.BlockSpec((tb, tt, d), lambda i, j: (i, j, 0))
    ada_spec = pl.BlockSpec((tb, 1, ada3.shape[2]), lambda i, j: (i, 0, 0))
    consts = (wts["w_out_fox"], wts["w_out_gla"], wts["w_out"], wts["norm_mix_post"])
    x1 = pl.pallas_call(
        _mix_out_kernel,
        out_shape=jax.ShapeDtypeStruct((bsz, t, d), F32),
        grid=(bsz // tb, n_t),
        in_specs=[x_spec, ada_spec, tok(FOX_WIDTH), tok(GLA_VAL_DIM), tok(2 * d)] + [full(a) for a in consts],
        out_specs=x_spec,
        compiler_params=_params("arbitrary", "arbitrary"),
        name="mix_out",
    )(x, ada3, y_a, o_b, gates, *consts)

    d_ff = wts["w_ffn_out"].shape[0]
    fc = 256 if d_ff % 256 == 0 else d_ff
    consts = (wts["norm_ffn_pre"], wts["w_ffn_in"], wts["w_ffn_out"], wts["norm_ffn_post"])
    return pl.pallas_call(
        functools.partial(_ffn_kernel, d_ff=d_ff, fc=fc),
        out_shape=jax.ShapeDtypeStruct((bsz, t, d), F32),
        grid=(bsz // tb, n_t),
        in_specs=[x_spec, ada_spec] + [full(a) for a in consts],
        out_specs=x_spec,
        compiler_params=_params("arbitrary", "arbitrary"),
        name="ffn",
    )(x1, ada3, *consts)


def _prep_weights(layer, norm_mix_pre, norm_mix_post, norm_ffn_pre, norm_ffn_post, w_in, b_forget,
                  w_alpha_up, b_alpha, gla_norm, w_out_fox, w_out_gla, w_out, w_ffn_in, w_ffn_out):
    d = w_in.shape[1]
    w = w_in[layer]
    o_f = 3 * FOX_WIDTH
    o_b = o_f + FOX_HEADS
    o_a = o_b + 2 * GLA_KEY_DIM + 2 * GLA_VAL_DIM
    o_g = o_a + GLA_GATE_RANK
    ws = jnp.zeros((d, 2 * LANES), F32)
    ws = ws.at[:, 0:FOX_HEADS].set(w[:, o_f:o_b])
    ws = ws.at[:, LANES:LANES + GLA_GATE_RANK].set(w[:, o_a:o_g])
    return {
        "wa": w[:, 0:o_f].astype(BF16),
        "ws": ws.astype(BF16),
        "wb": w[:, o_b:o_a].astype(BF16),
        "wg": w[:, o_g:].astype(BF16),
        "bf": jnp.zeros((1, LANES), F32).at[0, 0:FOX_HEADS].set(b_forget[layer]),
        "wup": jnp.zeros((LANES, GLA_KEY_DIM), F32).at[0:GLA_GATE_RANK].set(w_alpha_up[layer]),
        "bal": b_alpha[layer][None, :],
        "norm_mix_pre": norm_mix_pre[layer][None, :],
        "norm_mix_post": norm_mix_post[layer][None, :],
        "norm_ffn_pre": norm_ffn_pre[layer][None, :],
        "norm_ffn_post": norm_ffn_post[layer][None, :],
        "gla_norm": gla_norm[layer][None, :],
        "w_out_fox": w_out_fox[layer].astype(BF16),
        "w_out_gla": w_out_gla[layer].astype(BF16),
        "w_out": w_out[layer].astype(BF16),
        "w_ffn_in": w_ffn_in[layer].astype(BF16),
        "w_ffn_out": w_ffn_out[layer].astype(BF16),
    }


def _tiles(bsz, t):
    if t >= TOKEN_TILE:
        return 1, TOKEN_TILE
    return min(bsz, TOKEN_TILE // t), t


def kernel(x_prompt, x_sample, c_prompt, c_sample, cache_k, cache_v, cache_logf, state_gla, page_table, w_ada, b_ada, norm_mix_pre, norm_mix_post, norm_ffn_pre, norm_ffn_post, w_in, b_forget, w_alpha_up, b_alpha, gla_norm, w_out_fox, w_out_gla, w_out, w_ffn_in, w_ffn_out):
    depth = w_in.shape[0]
    bp, tp, d = x_prompt.shape
    bs, ts, _ = x_sample.shape
    yp, ys = x_prompt, x_sample
    c_all = jnp.concatenate([c_prompt, c_sample], axis=0)
    outs = [[] for _ in range(8)]
    for layer in range(depth):
        wts = _prep_weights(layer, norm_mix_pre, norm_mix_post, norm_ffn_pre, norm_ffn_post, w_in, b_forget,
                            w_alpha_up, b_alpha, gla_norm, w_out_fox, w_out_gla, w_out, w_ffn_in, w_ffn_out)
        ada = _ada_proj(c_all, w_ada[layer], b_ada[layer][None, :])
        ada_p = ada[0:bp].reshape(bp, 1, 6 * d)
        ada_s = ada[bp:].reshape(bs, 1, 6 * d)

        tb, tt = _tiles(bp, tp)
        (qa, kaf, kab, vaf, vab, logf, cum, qb, kb, vb, rb, la, gates) = _in_proj(
            yp, ada_p, wts["norm_mix_pre"], wts, tb=tb, tt=tt, act_dtype=BF16)
        y_a = _fox_prompt(qa, kab, vab, cum, bp, tp)
        o_b, s_new = _gla(qb, kb, vb, la, rb, wts["gla_norm"], None, bp, tp, math.gcd(tp, GLA_PROMPT_CHUNK))
        yp = _tail(yp, ada_p, y_a, o_b, gates, wts, tb=tb, tt=tt)
        outs[0].append(kaf.reshape(bp, tp, FOX_HEADS, FOX_HEAD_DIM))
        outs[1].append(vaf.reshape(bp, tp, FOX_HEADS, FOX_HEAD_DIM))
        outs[2].append(logf.reshape(bp, tp, FOX_HEADS))
        outs[3].append(s_new)

        tb, tt = _tiles(bs, ts)
        (qa, kaf, kab, vaf, vab, logf, cum, qb, kb, vb, rb, la, gates) = _in_proj(
            ys, ada_s, wts["norm_mix_pre"], wts, tb=tb, tt=tt, act_dtype=F32)
        logf_t = jnp.swapaxes(cache_logf[layer], 1, 2)
        y_a = _fox_sample(qa.reshape(bs, ts, FOX_WIDTH), cum.reshape(bs, ts, FOX_HEADS),
                          kaf.reshape(bs, ts, FOX_WIDTH), vaf.reshape(bs, ts, FOX_WIDTH),
                          logf.reshape(bs, ts, FOX_HEADS), cache_k[layer:layer + 1], cache_v[layer:layer + 1],
                          logf_t, page_table)
        o_b, s_new = _gla(qb, kb, vb, la, rb, wts["gla_norm"], state_gla[layer], bs, ts, ts)
        ys = _tail(ys, ada_s, y_a.astype(BF16), o_b, gates, wts, tb=tb, tt=tt)
        outs[4].append(kaf.reshape(bs, ts, FOX_HEADS, FOX_HEAD_DIM))
        outs[5].append(vaf.reshape(bs, ts, FOX_HEADS, FOX_HEAD_DIM))
        outs[6].append(logf.reshape(bs, ts, FOX_HEADS))
        outs[7].append(s_new)
    return (yp, ys) + tuple(jnp.stack(o) for o in outs)
```

```python
import functools
import math

import jax
import jax.numpy as jnp
from jax import lax
from jax.experimental import pallas as pl
from jax.experimental.pallas import tpu as pltpu

F32 = jnp.float32
BF16 = jnp.bfloat16

FOX_HEADS = 8
FOX_HEAD_DIM = 64
FOX_WIDTH = FOX_HEADS * FOX_HEAD_DIM
GLA_HEADS = 4
GLA_HK = 128
GLA_HV = 256
GLA_KEY_DIM = GLA_HEADS * GLA_HK
GLA_VAL_DIM = GLA_HEADS * GLA_HV
GLA_GATE_RANK = 16
GLA_GATE_TEMP = 16.0
EPS = 1e-6

LANES = 128
VMEM_LIMIT_BYTES = 56 * 1024 * 1024
TOKEN_TILE = 256
GLA_DIAG_BLOCK = 16
GLA_PROMPT_CHUNK = 128
FOX_Q_TILE = 256
FOX_K_TILE = 256
DECODE_PAGES_PER_STEP = 8
NEG_INF = float("-inf")


def _params(*sem):
    return pltpu.CompilerParams(dimension_semantics=sem, vmem_limit_bytes=VMEM_LIMIT_BYTES)


def _dot(a, b):
    return jnp.dot(a, b, preferred_element_type=F32)


def _dot_nt(a, b):
    return lax.dot_general(a, b, (((1,), (1,)), ((), ())), preferred_element_type=F32)


def _split3(x):
    hi = x.astype(BF16)
    r1 = x - hi.astype(F32)
    mid = r1.astype(BF16)
    lo = (r1 - mid.astype(F32)).astype(BF16)
    return hi, mid, lo


def _dot_exact_lhs01(l01, x):
    hi, mid, lo = _split3(x)
    return _dot(l01, hi) + _dot(l01, mid) + _dot(l01, lo)


def _log_sigmoid(x):
    return jnp.minimum(x, 0.0) - jnp.log1p(jnp.exp(-jnp.abs(x)))


def _sigmoid(x):
    return 1.0 / (1.0 + jnp.exp(-x))


def _rms(x):
    return x * lax.rsqrt(jnp.mean(x * x, axis=-1, keepdims=True) + EPS)


def _ada_kernel(c_ref, w_ref, b_ref, o_ref):
    o_ref[...] = _dot(c_ref[...].astype(BF16), w_ref[...].astype(BF16)) + b_ref[...]


def _ada_proj(c, w, b):
    n, d = c.shape
    cols = w.shape[1]
    bn = 1024
    return pl.pallas_call(
        _ada_kernel,
        out_shape=jax.ShapeDtypeStruct((n, cols), F32),
        grid=(cols // bn,),
        in_specs=[pl.BlockSpec((n, d), lambda j: (0, 0)),
                  pl.BlockSpec((d, bn), lambda j: (0, j)),
                  pl.BlockSpec((1, bn), lambda j: (0, j))],
        out_specs=pl.BlockSpec((n, bn), lambda j: (0, j)),
        compiler_params=_params("arbitrary"),
        name="ada_proj",
    )(c, w, b)


def _in_proj_kernel(x_ref, ada_ref, g_ref, wa_ref, ws_ref, wb_ref, wg_ref, bf_ref, wup_ref,
                    bal_ref, lmat_ref,
                    qa_ref, kaf_ref, kab_ref, vaf_ref, vab_ref, logf_ref, cum_ref,
                    qb_ref, kb_ref, vb_ref, rb_ref, la_ref, gate_ref, carry_ref,
                    *, carry_rows):
    tb, tt, d = x_ref.shape
    m = tb * tt
    x = x_ref[...]
    y = _rms(x) * g_ref[...]
    shift = ada_ref[:, :, 0:d]
    scale = ada_ref[:, :, d:2 * d]
    h = (y * (1.0 + scale) + shift).reshape(m, d).astype(BF16)

    za = _dot(h, wa_ref[...])
    w = FOX_WIDTH
    qa_ref[...] = (za[:, 0:w] * (FOX_HEAD_DIM ** -0.5)).astype(qa_ref.dtype)
    ka = za[:, w:2 * w]
    va = za[:, 2 * w:3 * w]
    kaf_ref[...] = ka
    kab_ref[...] = ka.astype(BF16)
    vaf_ref[...] = va
    vab_ref[...] = va.astype(BF16)

    zs = _dot(h, ws_ref[...])
    logf = _log_sigmoid(zs[:, 0:LANES] + bf_ref[...])
    logf_ref[...] = logf[:, 0:FOX_HEADS]
    cum = _dot_exact_lhs01(lmat_ref[...], logf)
    if carry_rows:
        @pl.when(pl.program_id(1) == 0)
        def _():
            carry_ref[...] = jnp.zeros_like(carry_ref)
        cum = cum + carry_ref[0:1, :]
        carry_ref[0:1, :] = cum[m - 1:m, :]
    cum_ref[...] = cum[:, 0:FOX_HEADS]

    a_low = zs[:, LANES:2 * LANES]
    a_hi = a_low.astype(BF16)
    a_mid = (a_low - a_hi.astype(F32)).astype(BF16)
    wup = wup_ref[...]
    w_hi = wup.astype(BF16)
    w_mid = (wup - w_hi.astype(F32)).astype(BF16)
    a_logit = _dot(a_hi, w_hi) + _dot(a_hi, w_mid) + _dot(a_mid, w_hi) + bal_ref[...]
    la_ref[...] = _log_sigmoid(a_logit) * (1.0 / GLA_GATE_TEMP)

    zb = _dot(h, wb_ref[...])
    kd, vd = GLA_KEY_DIM, GLA_VAL_DIM
    qb_ref[...] = zb[:, 0:kd].astype(qb_ref.dtype)
    kb_ref[...] = zb[:, kd:2 * kd].astype(kb_ref.dtype)
    vb_ref[...] = zb[:, 2 * kd:2 * kd + vd].astype(vb_ref.dtype)
    rb_ref[...] = zb[:, 2 * kd + vd:2 * kd + 2 * vd].astype(rb_ref.dtype)

    gate_ref[...] = _sigmoid(_dot(h, wg_ref[...])).astype(gate_ref.dtype)


def _in_proj(x, ada3, gain, wts, *, tb, tt, act_dtype):
    bsz, t, d = x.shape
    m = tb * tt
    n_t = t // tt
    n_tok = bsz * t
    carry_rows = n_t > 1
    assert (tb == 1) or (tt == t)
    row = lax.broadcasted_iota(jnp.int32, (m, m), 0)
    col = lax.broadcasted_iota(jnp.int32, (m, m), 1)
    lmat = ((col <= row) & (row // tt == col // tt)).astype(BF16)

    def full(a):
        return pl.BlockSpec(a.shape, lambda i, j: (0,) * a.ndim)

    def tok(cols):
        return pl.BlockSpec((m, cols), lambda i, j: (i * n_t + j, 0))

    out_cols = [(FOX_WIDTH, act_dtype), (FOX_WIDTH, F32), (FOX_WIDTH, BF16), (FOX_WIDTH, F32),
                (FOX_WIDTH, BF16), (FOX_HEADS, F32), (FOX_HEADS, F32),
                (GLA_KEY_DIM, act_dtype), (GLA_KEY_DIM, act_dtype), (GLA_VAL_DIM, act_dtype),
                (GLA_VAL_DIM, act_dtype), (GLA_KEY_DIM, F32), (2 * d, BF16)]
    consts = (gain, wts["wa"], wts["ws"], wts["wb"], wts["wg"], wts["bf"], wts["wup"], wts["bal"], lmat)
    return pl.pallas_call(
        functools.partial(_in_proj_kernel, carry_rows=carry_rows),
        out_shape=[jax.ShapeDtypeStruct((n_tok, c), dt) for c, dt in out_cols],
        grid=(bsz // tb, n_t),
        in_specs=[pl.BlockSpec((tb, tt, d), lambda i, j: (i, j, 0)),
                  pl.BlockSpec((tb, 1, ada3.shape[2]), lambda i, j: (i, 0, 0))]
                 + [full(a) for a in consts],
        out_specs=[tok(c) for c, _ in out_cols],
        scratch_shapes=[pltpu.VMEM((8, LANES), F32)],
        compiler_params=_params("arbitrary", "arbitrary"),
        name="in_proj",
    )(x, ada3, *consts)


def _fox_prompt_kernel(q_ref, k_ref, v_ref, cq_ref, ck_ref, o_ref, *, tq, tk):
    qi = pl.program_id(2)
    q = q_ref[...]
    lane = lax.broadcasted_iota(jnp.int32, (1, LANES), 1)
    lo_half = lane < FOX_HEAD_DIM
    q_heads = (jnp.where(lo_half, q, jnp.zeros_like(q)), jnp.where(lo_half, jnp.zeros_like(q), q))
    cq = cq_ref[0, 0]
    cq_heads = (cq[:, 0:1], cq[:, 1:2])
    row = lax.broadcasted_iota(jnp.int32, (tq, tk), 0)
    col = lax.broadcasted_iota(jnp.int32, (tq, tk), 1)

    def block(ki, carry, masked):
        k0 = pl.multiple_of(ki * tk, tk)
        k = k_ref[pl.ds(k0, tk), :]
        v = v_ref[pl.ds(k0, tk), :]
        ck = ck_ref[0, 0, :, pl.ds(k0, tk)]
        out = []
        for g in range(2):
            m_prev, l_prev, acc_prev = carry[g]
            s = _dot_nt(q_heads[g], k) + cq_heads[g] - ck[g:g + 1, :]
            if masked:
                s = jnp.where(col <= row, s, NEG_INF)
            m_new = jnp.maximum(m_prev, jnp.max(s, axis=1, keepdims=True))
            alpha = jnp.exp(m_prev - m_new)
            p = jnp.exp(s - m_new)
            l_new = alpha * l_prev + jnp.sum(p, axis=1, keepdims=True)
            acc_new = alpha * acc_prev + _dot(p.astype(BF16), v)
            out.append((m_new, l_new, acc_new))
        return tuple(out)

    init = tuple((jnp.full((tq, 1), NEG_INF, F32), jnp.zeros((tq, 1), F32),
                  jnp.zeros((tq, LANES), F32)) for _ in range(2))
    carry = lax.fori_loop(0, qi, lambda ki, c: block(ki, c, False), init)
    carry = block(qi, carry, True)
    (m0, l0, a0), (m1, l1, a1) = carry
    o_ref[...] = jnp.where(lo_half, a0 / l0, a1 / l1).astype(o_ref.dtype)


def _fox_prompt(q, k, v, cum, bsz, t):
    tq = min(FOX_Q_TILE, t)
    tk = tq
    n_q = t // tq
    pairs = FOX_HEADS // 2
    cum4 = cum.reshape(bsz, t, pairs, 2)
    cq = cum4.transpose(0, 2, 1, 3)
    ck = cum4.transpose(0, 2, 3, 1)
    return pl.pallas_call(
        functools.partial(_fox_prompt_kernel, tq=tq, tk=tk),
        out_shape=jax.ShapeDtypeStruct((bsz * t, FOX_WIDTH), BF16),
        grid=(bsz, pairs, n_q),
        in_specs=[pl.BlockSpec((tq, LANES), lambda b, hp, i: (b * n_q + i, hp)),
                  pl.BlockSpec((t, LANES), lambda b, hp, i: (b, hp)),
                  pl.BlockSpec((t, LANES), lambda b, hp, i: (b, hp)),
                  pl.BlockSpec((1, 1, tq, 2), lambda b, hp, i: (b, hp, i, 0)),
                  pl.BlockSpec((1, 1, 2, t), lambda b, hp, i: (b, hp, 0, 0))],
        out_specs=pl.BlockSpec((tq, LANES), lambda b, hp, i: (b * n_q + i, hp)),
        compiler_params=_params("arbitrary", "arbitrary", "arbitrary"),
        name="fox_prompt",
    )(q, k, v, cq, ck)


def _lane_cumsum(x):
    lane = lax.broadcasted_iota(jnp.int32, x.shape, 1)
    sh = 1
    while sh < LANES:
        x = x + jnp.where(lane >= sh, pltpu.roll(x, sh, 1), 0.0)
        sh *= 2
    return x


def _fox_sample_kernel(pt_ref, q_ref, cq_ref, *refs, pages, n_tok):
    k_refs = refs[0:pages]
    v_refs = refs[pages:2 * pages]
    f_refs = refs[2 * pages:3 * pages]
    kn_ref, vn_ref, fn_ref, o_ref, m_ref, l_ref, acc_ref, carry_ref = refs[3 * pages:]
    j = pl.program_id(1)
    rows = FOX_HEADS * n_tok

    @pl.when(j == 0)
    def _():
        m_ref[...] = jnp.full_like(m_ref, NEG_INF)
        l_ref[...] = jnp.zeros_like(l_ref)
        acc_ref[...] = jnp.zeros_like(acc_ref)
        carry_ref[...] = jnp.zeros_like(carry_ref)

    qbd = q_ref[0]
    cq = cq_ref[0]

    def per_row(x):
        return jnp.concatenate([jnp.broadcast_to(x[h:h + 1, :], (n_tok, LANES)) for h in range(FOX_HEADS)], axis=0)

    def attend(kt_list, vt_list, bias_list, mask):
        parts = [_dot(qbd, kt.astype(BF16)) + per_row(bias) for kt, bias in zip(kt_list, bias_list)]
        s = (jnp.concatenate(parts, axis=1) if len(parts) > 1 else parts[0]) + cq
        if mask is not None:
            s = jnp.where(mask, s, NEG_INF)
        m_prev = m_ref[...]
        m_new = jnp.maximum(m_prev, jnp.max(s, axis=1, keepdims=True))
        alpha = jnp.exp(m_prev - m_new)
        p = jnp.exp(s - m_new)
        l_ref[...] = alpha * l_ref[...] + jnp.sum(p, axis=1, keepdims=True)
        m_ref[...] = m_new
        pv = None
        for g, vt in enumerate(vt_list):
            d = _dot_nt(p[:, g * LANES:(g + 1) * LANES].astype(BF16), vt.astype(BF16))
            pv = d if pv is None else pv + d
        acc_ref[...] = alpha * acc_ref[...] + pv

    bias_list = []
    carry = carry_ref[...]
    for r in f_refs:
        lf = r[0]
        bias_list.append(-(_lane_cumsum(lf) + carry))
        carry = carry + jnp.sum(lf, axis=1, keepdims=True)
    carry_ref[...] = carry
    attend([r[0] for r in k_refs], [r[0] for r in v_refs], bias_list, None)

    @pl.when(j == pl.num_programs(1) - 1)
    def _():
        total = carry_ref[...]
        cn = _lane_cumsum(fn_ref[0])
        row = lax.broadcasted_iota(jnp.int32, (rows, LANES), 0)
        lane = lax.broadcasted_iota(jnp.int32, (rows, LANES), 1)
        mask = (lane <= row % n_tok) & (lane < n_tok)
        attend([kn_ref[0]], [vn_ref[0]], [-(cn + total)], mask)
        res = acc_ref[...] / l_ref[...]
        rowh = lax.broadcasted_iota(jnp.int32, (rows, FOX_WIDTH), 0) // n_tok
        colh = lax.broadcasted_iota(jnp.int32, (rows, FOX_WIDTH), 1) // FOX_HEAD_DIM
        res = jnp.where(rowh == colh, res, 0.0)
        y = res[0:n_tok, :]
        for h in range(1, FOX_HEADS):
            y = y + res[h * n_tok:(h + 1) * n_tok, :]
        o_ref[0] = y


def _fox_sample(q, cq, k_new, v_new, logf_new, cache_kt, cache_vt, cache_logf_t, page_table):
    db, n_tok, _ = q.shape
    n_pages = page_table.shape[1]
    page = cache_kt.shape[2]
    assert page == LANES
    pages = math.gcd(DECODE_PAGES_PER_STEP, n_pages)
    rows = FOX_HEADS * n_tok
    h, dh = FOX_HEADS, FOX_HEAD_DIM
    q4 = q.reshape(db, n_tok, h, dh).transpose(0, 2, 1, 3)
    eye = jnp.eye(h, dtype=F32)
    qbd = (q4[:, :, :, None, :] * eye[None, :, None, :, None]).reshape(db, rows, FOX_WIDTH).astype(BF16)
    cq_col = cq.transpose(0, 2, 1).reshape(db, rows, 1)

    def new_page(a):
        return jnp.pad(a.transpose(0, 2, 1), ((0, 0), (0, 0), (0, page - n_tok)))

    kn = new_page(k_new)
    vn = new_page(v_new)
    fn = new_page(logf_new)

    def paged(r, g):
        return pl.BlockSpec((1, r, page), lambda b, j, pt: (pt[b * n_pages + j * pages + g], 0, 0))

    def per_batch(r, c):
        return pl.BlockSpec((1, r, c), lambda b, j, pt: (b, 0, 0))

    grid_spec = pltpu.PrefetchScalarGridSpec(
        num_scalar_prefetch=1,
        grid=(db, n_pages // pages),
        in_specs=[per_batch(rows, FOX_WIDTH), per_batch(rows, 1)]
                 + [paged(FOX_WIDTH, g) for g in range(pages)]
                 + [paged(FOX_WIDTH, g) for g in range(pages)]
                 + [paged(h, g) for g in range(pages)]
                 + [per_batch(FOX_WIDTH, page), per_batch(FOX_WIDTH, page), per_batch(h, page)],
        out_specs=per_batch(n_tok, FOX_WIDTH),
        scratch_shapes=[pltpu.VMEM((rows, 1), F32), pltpu.VMEM((rows, 1), F32),
                        pltpu.VMEM((rows, FOX_WIDTH), F32), pltpu.VMEM((h, 1), F32)],
    )
    out = pl.pallas_call(
        functools.partial(_fox_sample_kernel, pages=pages, n_tok=n_tok),
        out_shape=jax.ShapeDtypeStruct((db, n_tok, FOX_WIDTH), F32),
        grid_spec=grid_spec,
        compiler_params=_params("arbitrary", "arbitrary"),
        name="fox_sample",
    )(page_table.reshape(-1), qbd, cq_col,
      *([cache_kt] * pages), *([cache_vt] * pages), *([cache_logf_t] * pages), kn, vn, fn)
    return out.reshape(db * n_tok, FOX_WIDTH)


def _gla_spans(lo, hi, blk):
    if hi - lo <= blk:
        return []
    mid = (lo + hi) // 2
    return [(lo, mid, hi)] + _gla_spans(lo, mid, blk) + _gla_spans(mid, hi, blk)


def _gla_kernel(q_ref, k_ref, v_ref, la_ref, r_ref, gain_ref, *rest, chunk, blk, has_s0):
    if has_s0:
        s0_ref, o_ref, sout_ref, st_ref = rest
    else:
        o_ref, sout_ref, st_ref = rest
    c = pl.program_id(1)
    cp = max(chunk, LANES)
    xr = -(-(chunk + 8) // LANES) * LANES

    @pl.when(c == 0)
    def _():
        for hd in range(GLA_HEADS):
            if has_s0:
                st_ref[hd] = s0_ref[0, hd]
            else:
                st_ref[hd] = jnp.zeros((GLA_HK, GLA_HV), F32)

    la = la_ref[...]
    if chunk <= 8:
        b = jnp.zeros_like(la)
        rowi = lax.broadcasted_iota(jnp.int32, (chunk, 1), 0)
        for s in range(chunk):
            b = b + jnp.where(rowi >= s, la[s:s + 1, :], 0.0)
    else:
        ri = lax.broadcasted_iota(jnp.int32, (chunk, chunk), 0)
        ci = lax.broadcasted_iota(jnp.int32, (chunk, chunk), 1)
        b = _dot_exact_lhs01((ci <= ri).astype(BF16), la)

    spans = _gla_spans(0, chunk, blk)
    scale = GLA_HK ** -0.5
    ti = lax.broadcasted_iota(jnp.int32, (blk, blk, 1), 0)
    si = lax.broadcasted_iota(jnp.int32, (blk, blk, 1), 1)
    tri = si <= ti
    gain = gain_ref[...]

    for hd in range(GLA_HEADS):
        ks = slice(hd * GLA_HK, (hd + 1) * GLA_HK)
        vs = slice(hd * GLA_HV, (hd + 1) * GLA_HV)
        bh = b[:, ks]
        qh = q_ref[:, ks].astype(F32) * scale
        kh = k_ref[:, ks].astype(F32)
        vh = v_ref[:, vs].astype(F32)
        b_last = bh[chunk - 1:chunk, :]
        st = st_ref[hd]

        q0 = (qh * jnp.exp(bh)).astype(BF16)
        o = _dot(q0, st.astype(BF16))

        if spans:
            q_parts, k_parts = [], []
            for lo, mid, hi in spans:
                bref = bh[mid - 1:mid, :]
                qg = qh[mid:hi] * jnp.exp(bh[mid:hi] - bref)
                kg = kh[lo:mid] * jnp.exp(bref - bh[lo:mid])
                qz = [jnp.zeros((mid, GLA_HK), F32), qg]
                if chunk > hi:
                    qz.append(jnp.zeros((chunk - hi, GLA_HK), F32))
                kz = ([jnp.zeros((lo, GLA_HK), F32)] if lo > 0 else []) + [kg, jnp.zeros((chunk - mid, GLA_HK), F32)]
                q_parts.append(jnp.concatenate(qz, axis=0))
                k_parts.append(jnp.concatenate(kz, axis=0))
            q_hat = jnp.concatenate(q_parts, axis=1).astype(BF16)
            k_hat = jnp.concatenate(k_parts, axis=1).astype(BF16)
            a_mat = _dot_nt(q_hat, k_hat)
        else:
            a_mat = None
        diag_rows = []
        for i in range(chunk // blk):
            r0 = i * blk
            qi, ki, bi = qh[r0:r0 + blk], kh[r0:r0 + blk], bh[r0:r0 + blk]
            dec = jnp.where(tri, jnp.exp(bi[:, None, :] - bi[None, :, :]), 0.0)
            a_ii = jnp.sum(qi[:, None, :] * ki[None, :, :] * dec, axis=-1)
            if chunk == blk:
                diag_rows.append(a_ii)
            else:
                pieces = ([jnp.zeros((blk, r0), F32)] if r0 > 0 else []) + [a_ii]
                if chunk - r0 - blk > 0:
                    pieces.append(jnp.zeros((blk, chunk - r0 - blk), F32))
                diag_rows.append(jnp.concatenate(pieces, axis=1))
        a_diag = jnp.concatenate(diag_rows, axis=0) if len(diag_rows) > 1 else diag_rows[0]
        a_mat = a_diag if a_mat is None else a_mat + a_diag

        v_pad = vh
        if cp > chunk:
            v_pad = jnp.concatenate([vh, jnp.zeros((cp - chunk, GLA_HV), F32)], axis=0)
            a_mat = jnp.concatenate([a_mat, jnp.zeros((chunk, cp - chunk), F32)], axis=1)
        v_pad = v_pad.astype(BF16)
        o = o + _dot(a_mat.astype(BF16), v_pad)

        k0 = kh * jnp.exp(b_last - bh)
        row8 = lax.broadcasted_iota(jnp.int32, (8, GLA_HK), 0)
        dec_rows = jnp.where(row8 == 0, jnp.broadcast_to(jnp.exp(b_last), (8, GLA_HK)), 0.0)
        x_t = jnp.concatenate([k0, dec_rows, jnp.zeros((xr - chunk - 8, GLA_HK), F32)], axis=0).T
        k0_t = x_t[:, 0:cp].astype(BF16)
        st_ref[hd] = st * x_t[:, chunk:chunk + 1] + _dot(k0_t, v_pad)

        r = r_ref[:, vs].astype(F32)
        o_ref[:, vs] = (_rms(o) * gain * (r * _sigmoid(r))).astype(o_ref.dtype)

    @pl.when(c == pl.num_programs(1) - 1)
    def _():
        for hd in range(GLA_HEADS):
            sout_ref[0, hd] = st_ref[hd]


def _gla(q, k, v, la, r, gain, s0, bsz, t, chunk):
    n_c = t // chunk
    blk = min(GLA_DIAG_BLOCK, chunk)
    has_s0 = s0 is not None

    def tok(cols):
        return pl.BlockSpec((chunk, cols), lambda b, c: (b * n_c + c, 0))

    state_spec = pl.BlockSpec((1, GLA_HEADS, GLA_HK, GLA_HV), lambda b, c: (b, 0, 0, 0))
    in_specs = [tok(GLA_KEY_DIM), tok(GLA_KEY_DIM), tok(GLA_VAL_DIM), tok(GLA_KEY_DIM), tok(GLA_VAL_DIM),
                pl.BlockSpec((1, GLA_HV), lambda b, c: (0, 0))]
    args = [q, k, v, la, r, gain]
    if has_s0:
        in_specs.append(state_spec)
        args.append(s0)
    return pl.pallas_call(
        functools.partial(_gla_kernel, chunk=chunk, blk=blk, has_s0=has_s0),
        out_shape=[jax.ShapeDtypeStruct((bsz * t, GLA_VAL_DIM), BF16),
                   jax.ShapeDtypeStruct((bsz, GLA_HEADS, GLA_HK, GLA_HV), F32)],
        grid=(bsz, n_c),
        in_specs=in_specs,
        out_specs=[tok(GLA_VAL_DIM), state_spec],
        scratch_shapes=[pltpu.VMEM((GLA_HEADS, GLA_HK, GLA_HV), F32)],
        compiler_params=_params("arbitrary", "arbitrary"),
        name="gla",
    )(*args)


def _mix_out_kernel(x_ref, ada_ref, ya_ref, ob_ref, gate_ref, wfox_ref, wgla_ref, wout_ref, g_ref, o_ref):
    tb, tt, d = x_ref.shape
    m = tb * tt
    gates = gate_ref[...].astype(F32)
    merged = (gates[:, 0:d] * _dot(ya_ref[...], wfox_ref[...])
              + gates[:, d:2 * d] * _dot(ob_ref[...], wgla_ref[...]))
    y = _dot(merged.astype(BF16), wout_ref[...])
    y = (_rms(y) * g_ref[...]).reshape(tb, tt, d)
    o_ref[...] = x_ref[...] + ada_ref[:, :, 2 * d:3 * d] * y


def _ffn_kernel(x_ref, ada_ref, gpre_ref, win_ref, wout_ref, gpost_ref, o_ref, *, d_ff, fc):
    tb, tt, d = x_ref.shape
    m = tb * tt
    x = x_ref[...]
    y = _rms(x) * gpre_ref[...]
    h = (y * (1.0 + ada_ref[:, :, 4 * d:5 * d]) + ada_ref[:, :, 3 * d:4 * d]).reshape(m, d).astype(BF16)
    acc = jnp.zeros((m, d), F32)
    for c0 in range(0, d_ff, fc):
        up = _dot(h, win_ref[:, c0:c0 + fc])
        gt = _dot(h, win_ref[:, d_ff + c0:d_ff + c0 + fc])
        acc = acc + _dot((gt * _sigmoid(gt) * up).astype(BF16), wout_ref[c0:c0 + fc, :])
    f = (_rms(acc) * gpost_ref[...]).reshape(tb, tt, d)
    o_ref[...] = x + ada_ref[:, :, 5 * d:6 * d] * f


def _tail(x, ada3, y_a, o_b, gates, wts, *, tb, tt):
    bsz, t, d = x.shape
    m = tb * tt
    n_t = t // tt

    def full(a):
        return pl.BlockSpec(a.shape, lambda i, j: (0,) * a.ndim)

    def tok(cols):
        return pl.BlockSpec((m, cols), lambda i, j: (i * n_t + j, 0))

    x_spec = pl.BlockSpec((tb, tt, d), lambda i, j: (i, j, 0))
    ada_spec = pl.BlockSpec((tb, 1, ada3.shape[2]), lambda i, j: (i, 0, 0))
    consts = (wts["w_out_fox"], wts["w_out_gla"], wts["w_out"], wts["norm_mix_post"])
    x1 = pl.pallas_call(
        _mix_out_kernel,
        out_shape=jax.ShapeDtypeStruct((bsz, t, d), F32),
        grid=(bsz // tb, n_t),
        in_specs=[x_spec, ada_spec, tok(FOX_WIDTH), tok(GLA_VAL_DIM), tok(2 * d)] + [full(a) for a in consts],
        out_specs=x_spec,
        compiler_params=_params("arbitrary", "arbitrary"),
        name="mix_out",
    )(x, ada3, y_a, o_b, gates, *consts)

    d_ff = wts["w_ffn_out"].shape[0]
    fc = 256 if d_ff % 256 == 0 else d_ff
    consts = (wts["norm_ffn_pre"], wts["w_ffn_in"], wts["w_ffn_out"], wts["norm_ffn_post"])
    return pl.pallas_call(
        functools.partial(_ffn_kernel, d_ff=d_ff, fc=fc),
        out_shape=jax.ShapeDtypeStruct((bsz, t, d), F32),
        grid=(bsz // tb, n_t),
        in_specs=[x_spec, ada_spec] + [full(a) for a in consts],
        out_specs=x_spec,
        compiler_params=_params("arbitrary", "arbitrary"),
        name="ffn",
    )(x1, ada3, *consts)


def _prep_weights(layer, norm_mix_pre, norm_mix_post, norm_ffn_pre, norm_ffn_post, w_in, b_forget,
                  w_alpha_up, b_alpha, gla_norm, w_out_fox, w_out_gla, w_out, w_ffn_in, w_ffn_out):
    d = w_in.shape[1]
    w = w_in[layer]
    o_f = 3 * FOX_WIDTH
    o_b = o_f + FOX_HEADS
    o_a = o_b + 2 * GLA_KEY_DIM + 2 * GLA_VAL_DIM
    o_g = o_a + GLA_GATE_RANK
    ws = jnp.zeros((d, 2 * LANES), F32)
    ws = ws.at[:, 0:FOX_HEADS].set(w[:, o_f:o_b])
    ws = ws.at[:, LANES:LANES + GLA_GATE_RANK].set(w[:, o_a:o_g])
    return {
        "wa": w[:, 0:o_f].astype(BF16),
        "ws": ws.astype(BF16),
        "wb": w[:, o_b:o_a].astype(BF16),
        "wg": w[:, o_g:].astype(BF16),
        "bf": jnp.zeros((1, LANES), F32).at[0, 0:FOX_HEADS].set(b_forget[layer]),
        "wup": jnp.zeros((LANES, GLA_KEY_DIM), F32).at[0:GLA_GATE_RANK].set(w_alpha_up[layer]),
        "bal": b_alpha[layer][None, :],
        "norm_mix_pre": norm_mix_pre[layer][None, :],
        "norm_mix_post": norm_mix_post[layer][None, :],
        "norm_ffn_pre": norm_ffn_pre[layer][None, :],
        "norm_ffn_post": norm_ffn_post[layer][None, :],
        "gla_norm": gla_norm[layer][None, :],
        "w_out_fox": w_out_fox[layer].astype(BF16),
        "w_out_gla": w_out_gla[layer].astype(BF16),
        "w_out": w_out[layer].astype(BF16),
        "w_ffn_in": w_ffn_in[layer].astype(BF16),
        "w_ffn_out": w_ffn_out[layer].astype(BF16),
    }


def _tiles(bsz, t):
    if t >= TOKEN_TILE:
        return 1, TOKEN_TILE
    return min(bsz, TOKEN_TILE // t), t


def kernel(x_prompt, x_sample, c_prompt, c_sample, cache_k, cache_v, cache_logf, state_gla, page_table, w_ada, b_ada, norm_mix_pre, norm_mix_post, norm_ffn_pre, norm_ffn_post, w_in, b_forget, w_alpha_up, b_alpha, gla_norm, w_out_fox, w_out_gla, w_out, w_ffn_in, w_ffn_out):
    depth = w_in.shape[0]
    bp, tp, d = x_prompt.shape
    bs, ts, _ = x_sample.shape
    yp, ys = x_prompt, x_sample
    c_all = jnp.concatenate([c_prompt, c_sample], axis=0)
    outs = [[] for _ in range(8)]
    for layer in range(depth):
        wts = _prep_weights(layer, norm_mix_pre, norm_mix_post, norm_ffn_pre, norm_ffn_post, w_in, b_forget,
                            w_alpha_up, b_alpha, gla_norm, w_out_fox, w_out_gla, w_out, w_ffn_in, w_ffn_out)
        ada = _ada_proj(c_all, w_ada[layer], b_ada[layer][None, :])
        ada_p = ada[0:bp].reshape(bp, 1, 6 * d)
        ada_s = ada[bp:].reshape(bs, 1, 6 * d)

        tb, tt = _tiles(bp, tp)
        (qa, kaf, kab, vaf, vab, logf, cum, qb, kb, vb, rb, la, gates) = _in_proj(
            yp, ada_p, wts["norm_mix_pre"], wts, tb=tb, tt=tt, act_dtype=BF16)
        y_a = _fox_prompt(qa, kab, vab, cum, bp, tp)
        o_b, s_new = _gla(qb, kb, vb, la, rb, wts["gla_norm"], None, bp, tp, math.gcd(tp, GLA_PROMPT_CHUNK))
        yp = _tail(yp, ada_p, y_a, o_b, gates, wts, tb=tb, tt=tt)
        outs[0].append(kaf.reshape(bp, tp, FOX_HEADS, FOX_HEAD_DIM))
        outs[1].append(vaf.reshape(bp, tp, FOX_HEADS, FOX_HEAD_DIM))
        outs[2].append(logf.reshape(bp, tp, FOX_HEADS))
        outs[3].append(s_new)

        tb, tt = _tiles(bs, ts)
        (qa, kaf, kab, vaf, vab, logf, cum, qb, kb, vb, rb, la, gates) = _in_proj(
            ys, ada_s, wts["norm_mix_pre"], wts, tb=tb, tt=tt, act_dtype=F32)
        pool, page = cache_k.shape[1], cache_k.shape[2]
        cache_kt = cache_k[layer].transpose(0, 2, 3, 1).reshape(pool, FOX_WIDTH, page)
        cache_vt = cache_v[layer].transpose(0, 2, 3, 1).reshape(pool, FOX_WIDTH, page)
        logf_t = cache_logf[layer].transpose(0, 2, 1)
        y_a = _fox_sample(qa.reshape(bs, ts, FOX_WIDTH), cum.reshape(bs, ts, FOX_HEADS),
                          kaf.reshape(bs, ts, FOX_WIDTH), vaf.reshape(bs, ts, FOX_WIDTH),
                          logf.reshape(bs, ts, FOX_HEADS), cache_kt, cache_vt, logf_t, page_table)
        o_b, s_new = _gla(qb, kb, vb, la, rb, wts["gla_norm"], state_gla[layer], bs, ts, ts)
        ys = _tail(ys, ada_s, y_a.astype(BF16), o_b, gates, wts, tb=tb, tt=tt)
        outs[4].append(kaf.reshape(bs, ts, FOX_HEADS, FOX_HEAD_DIM))
        outs[5].append(vaf.reshape(bs, ts, FOX_HEADS, FOX_HEAD_DIM))
        outs[6].append(logf.reshape(bs, ts, FOX_HEADS))
        outs[7].append(s_new)
    return (yp, ys) + tuple(jnp.stack(o) for o in outs)
```

```python
import functools
import math

import jax
import jax.numpy as jnp
from jax import lax
from jax.experimental import pallas as pl
from jax.experimental.pallas import tpu as pltpu

F32 = jnp.float32
BF16 = jnp.bfloat16

FOX_HEADS = 8
FOX_HEAD_DIM = 64
FOX_WIDTH = FOX_HEADS * FOX_HEAD_DIM
GLA_HEADS = 4
GLA_HK = 128
GLA_HV = 256
GLA_KEY_DIM = GLA_HEADS * GLA_HK
GLA_VAL_DIM = GLA_HEADS * GLA_HV
GLA_GATE_RANK = 16
GLA_GATE_TEMP = 16.0
EPS = 1e-6

LANES = 128
VMEM_LIMIT_BYTES = 56 * 1024 * 1024
TOKEN_TILE = 512
GLA_DIAG_BLOCK = 16
GLA_PROMPT_CHUNK = 128
FOX_Q_TILE = 1024
DECODE_PAGES_PER_STEP = 16
NEG_INF = float("-inf")


def _params(*sem):
    return pltpu.CompilerParams(dimension_semantics=sem, vmem_limit_bytes=VMEM_LIMIT_BYTES)


def _dot(a, b):
    return jnp.dot(a, b, preferred_element_type=F32)


def _dot_nt(a, b):
    return lax.dot_general(a, b, (((1,), (1,)), ((), ())), preferred_element_type=F32)


def _split3(x):
    hi = x.astype(BF16)
    r1 = x - hi.astype(F32)
    mid = r1.astype(BF16)
    lo = (r1 - mid.astype(F32)).astype(BF16)
    return hi, mid, lo


def _dot_exact_lhs01(l01, x):
    hi, mid, lo = _split3(x)
    return _dot(l01, hi) + _dot(l01, mid) + _dot(l01, lo)


def _log_sigmoid(x):
    return jnp.minimum(x, 0.0) - jnp.log1p(jnp.exp(-jnp.abs(x)))


def _sigmoid(x):
    return 1.0 / (1.0 + jnp.exp(-x))


def _rms(x):
    return x * lax.rsqrt(jnp.mean(x * x, axis=-1, keepdims=True) + EPS)


def _ada_kernel(c_ref, w_ref, b_ref, o_ref):
    o_ref[...] = _dot(c_ref[...].astype(BF16), w_ref[...].astype(BF16)) + b_ref[...]


def _ada_proj(c, w, b):
    n, d = c.shape
    cols = w.shape[1]
    bn = 1024
    return pl.pallas_call(
        _ada_kernel,
        out_shape=jax.ShapeDtypeStruct((n, cols), F32),
        grid=(cols // bn,),
        in_specs=[pl.BlockSpec((n, d), lambda j: (0, 0)),
                  pl.BlockSpec((d, bn), lambda j: (0, j)),
                  pl.BlockSpec((1, bn), lambda j: (0, j))],
        out_specs=pl.BlockSpec((n, bn), lambda j: (0, j)),
        compiler_params=_params("arbitrary"),
        name="ada_proj",
    )(c, w, b)


def _in_proj_kernel(x_ref, ada_ref, g_ref, wa_ref, ws_ref, wb_ref, wg_ref, bf_ref, wup_ref,
                    bal_ref, lmat_ref,
                    qa_ref, kaf_ref, kab_ref, vaf_ref, vab_ref, logf_ref, cum_ref,
                    qb_ref, kb_ref, vb_ref, rb_ref, la_ref, gate_ref, carry_ref,
                    *, carry_rows, kv_transposed):
    tb, tt, d = x_ref.shape
    m = tb * tt
    x = x_ref[...]
    y = _rms(x) * g_ref[...]
    shift = ada_ref[:, :, 0:d]
    scale = ada_ref[:, :, d:2 * d]
    h = (y * (1.0 + scale) + shift).reshape(m, d).astype(BF16)

    za = _dot(h, wa_ref[...])
    w = FOX_WIDTH
    qa_ref[...] = (za[:, 0:w] * (FOX_HEAD_DIM ** -0.5)).astype(qa_ref.dtype)
    ka = za[:, w:2 * w]
    va = za[:, 2 * w:3 * w]
    if kv_transposed:
        kaf_ref[0] = ka.T
        vaf_ref[0] = va.T
    else:
        kaf_ref[...] = ka
        vaf_ref[...] = va
    kab_ref[...] = ka.astype(BF16)
    vab_ref[...] = va.astype(BF16)

    zs = _dot(h, ws_ref[...])
    logf = _log_sigmoid(zs[:, 0:LANES] + bf_ref[...])
    logf_ref[...] = logf[:, 0:FOX_HEADS]
    cum = _dot_exact_lhs01(lmat_ref[...], logf)
    if carry_rows:
        @pl.when(pl.program_id(1) == 0)
        def _():
            carry_ref[...] = jnp.zeros_like(carry_ref)
        cum = cum + carry_ref[0:1, :]
        carry_ref[0:1, :] = cum[m - 1:m, :]
    cum_ref[...] = cum[:, 0:FOX_HEADS]

    a_low = zs[:, LANES:2 * LANES]
    a_hi = a_low.astype(BF16)
    a_mid = (a_low - a_hi.astype(F32)).astype(BF16)
    wup = wup_ref[...]
    w_hi = wup.astype(BF16)
    w_mid = (wup - w_hi.astype(F32)).astype(BF16)
    a_logit = _dot(a_hi, w_hi) + _dot(a_hi, w_mid) + _dot(a_mid, w_hi) + bal_ref[...]
    la_ref[...] = _log_sigmoid(a_logit) * (1.0 / GLA_GATE_TEMP)

    zb = _dot(h, wb_ref[...])
    kd, vd = GLA_KEY_DIM, GLA_VAL_DIM
    qb_ref[...] = zb[:, 0:kd].astype(qb_ref.dtype)
    kb_ref[...] = zb[:, kd:2 * kd].astype(kb_ref.dtype)
    vb_ref[...] = zb[:, 2 * kd:2 * kd + vd].astype(vb_ref.dtype)
    rb_ref[...] = zb[:, 2 * kd + vd:2 * kd + 2 * vd].astype(rb_ref.dtype)

    gate_ref[...] = _sigmoid(_dot(h, wg_ref[...])).astype(gate_ref.dtype)


def _in_proj(x, ada3, gain, wts, *, tb, tt, act_dtype):
    bsz, t, d = x.shape
    m = tb * tt
    n_t = t // tt
    n_tok = bsz * t
    carry_rows = n_t > 1
    assert (tb == 1) or (tt == t)
    kv_transposed = tb == 1 and tt % LANES == 0
    row = lax.broadcasted_iota(jnp.int32, (m, m), 0)
    col = lax.broadcasted_iota(jnp.int32, (m, m), 1)
    lmat = ((col <= row) & (row // tt == col // tt)).astype(BF16)

    def full(a):
        return pl.BlockSpec(a.shape, lambda i, j: (0,) * a.ndim)

    def tok(cols):
        return pl.BlockSpec((m, cols), lambda i, j: (i * n_t + j, 0))

    out_cols = [(FOX_WIDTH, act_dtype), (FOX_WIDTH, F32), (FOX_WIDTH, BF16), (FOX_WIDTH, F32),
                (FOX_WIDTH, BF16), (FOX_HEADS, F32), (FOX_HEADS, F32),
                (GLA_KEY_DIM, act_dtype), (GLA_KEY_DIM, act_dtype), (GLA_VAL_DIM, act_dtype),
                (GLA_VAL_DIM, act_dtype), (GLA_KEY_DIM, F32), (2 * d, BF16)]
    consts = (gain, wts["wa"], wts["ws"], wts["wb"], wts["wg"], wts["bf"], wts["wup"], wts["bal"], lmat)
    out_shape = [jax.ShapeDtypeStruct((n_tok, c), dt) for c, dt in out_cols]
    out_specs = [tok(c) for c, _ in out_cols]
    if kv_transposed:
        for idx in (1, 3):
            out_shape[idx] = jax.ShapeDtypeStruct((bsz, FOX_WIDTH, t), F32)
            out_specs[idx] = pl.BlockSpec((1, FOX_WIDTH, tt), lambda i, j: (i, 0, j))
    return pl.pallas_call(
        functools.partial(_in_proj_kernel, carry_rows=carry_rows, kv_transposed=kv_transposed),
        out_shape=out_shape,
        grid=(bsz // tb, n_t),
        in_specs=[pl.BlockSpec((tb, tt, d), lambda i, j: (i, j, 0)),
                  pl.BlockSpec((tb, 1, ada3.shape[2]), lambda i, j: (i, 0, 0))]
                 + [full(a) for a in consts],
        out_specs=out_specs,
        scratch_shapes=[pltpu.VMEM((8, LANES), F32)],
        compiler_params=_params("arbitrary", "arbitrary"),
        name="in_proj",
    )(x, ada3, *consts)


def _fox_prompt_kernel(q_ref, k_ref, v_ref, cq_ref, ck_ref, o_ref, *, tq, tk):
    qi = pl.program_id(2)
    q = q_ref[...]
    lane = lax.broadcasted_iota(jnp.int32, (1, LANES), 1)
    lo_half = lane < FOX_HEAD_DIM
    q_heads = (jnp.where(lo_half, q, jnp.zeros_like(q)), jnp.where(lo_half, jnp.zeros_like(q), q))
    cq = cq_ref[0, 0]
    cq_heads = (cq[:, 0:1], cq[:, 1:2])
    row = lax.broadcasted_iota(jnp.int32, (tq, tk), 0)
    col = lax.broadcasted_iota(jnp.int32, (tq, tk), 1)

    def block(ki, carry, masked):
        k0 = pl.multiple_of(ki * tk, tk)
        k = k_ref[pl.ds(k0, tk), :]
        v = v_ref[pl.ds(k0, tk), :]
        ck = ck_ref[0, 0, :, pl.ds(k0, tk)]
        out = []
        for g in range(2):
            m_prev, l_prev, acc_prev = carry[g]
            s = _dot_nt(q_heads[g], k) + cq_heads[g] - ck[g:g + 1, :]
            if masked:
                s = jnp.where(col <= row, s, NEG_INF)
            m_new = jnp.maximum(m_prev, jnp.max(s, axis=1, keepdims=True))
            alpha = jnp.exp(m_prev - m_new)
            p = jnp.exp(s - m_new)
            l_new = alpha * l_prev + jnp.sum(p, axis=1, keepdims=True)
            acc_new = alpha * acc_prev + _dot(p.astype(BF16), v)
            out.append((m_new, l_new, acc_new))
        return tuple(out)

    init = tuple((jnp.full((tq, 1), NEG_INF, F32), jnp.zeros((tq, 1), F32),
                  jnp.zeros((tq, LANES), F32)) for _ in range(2))
    carry = lax.fori_loop(0, qi, lambda ki, c: block(ki, c, False), init)
    carry = block(qi, carry, True)
    (m0, l0, a0), (m1, l1, a1) = carry
    o_ref[...] = jnp.where(lo_half, a0 / l0, a1 / l1).astype(o_ref.dtype)


def _fox_prompt(q, k, v, cum, bsz, t):
    tq = min(FOX_Q_TILE, t)
    tk = tq
    n_q = t // tq
    pairs = FOX_HEADS // 2
    cum4 = cum.reshape(bsz, t, pairs, 2)
    cq = cum4.transpose(0, 2, 1, 3)
    ck = cum4.transpose(0, 2, 3, 1)
    return pl.pallas_call(
        functools.partial(_fox_prompt_kernel, tq=tq, tk=tk),
        out_shape=jax.ShapeDtypeStruct((bsz * t, FOX_WIDTH), BF16),
        grid=(bsz, pairs, n_q),
        in_specs=[pl.BlockSpec((tq, LANES), lambda b, hp, i: (b * n_q + i, hp)),
                  pl.BlockSpec((t, LANES), lambda b, hp, i: (b, hp)),
                  pl.BlockSpec((t, LANES), lambda b, hp, i: (b, hp)),
                  pl.BlockSpec((1, 1, tq, 2), lambda b, hp, i: (b, hp, i, 0)),
                  pl.BlockSpec((1, 1, 2, t), lambda b, hp, i: (b, hp, 0, 0))],
        out_specs=pl.BlockSpec((tq, LANES), lambda b, hp, i: (b * n_q + i, hp)),
        compiler_params=_params("arbitrary", "arbitrary", "arbitrary"),
        name="fox_prompt",
    )(q, k, v, cq, ck)


def _lane_cumsum(x, u_incl):
    hi, mid, lo = _split3(x)
    return _dot(hi, u_incl) + _dot(mid, u_incl) + _dot(lo, u_incl)


def _fox_sample_kernel(pt_ref, q_ref, cq_ref, u_ref, *refs, pages, n_tok):
    k_refs = refs[0:pages]
    v_refs = refs[pages:2 * pages]
    f_refs = refs[2 * pages:3 * pages]
    kn_ref, vn_ref, fn_ref, o_ref, m_ref, l_ref, acc_ref, carry_ref = refs[3 * pages:]
    u_incl = u_ref[...]
    j = pl.program_id(1)
    rows = FOX_HEADS * n_tok

    @pl.when(j == 0)
    def _():
        m_ref[...] = jnp.full_like(m_ref, NEG_INF)
        l_ref[...] = jnp.zeros_like(l_ref)
        acc_ref[...] = jnp.zeros_like(acc_ref)
        carry_ref[...] = jnp.zeros_like(carry_ref)

    qbd = q_ref[0]
    cq = cq_ref[0]

    def per_row(x):
        return jnp.concatenate([jnp.broadcast_to(x[h:h + 1, :], (n_tok, LANES)) for h in range(FOX_HEADS)], axis=0)

    def attend(kt_list, vt_list, bias_list, mask):
        parts = [_dot(qbd, kt.astype(BF16)) + per_row(bias) for kt, bias in zip(kt_list, bias_list)]
        s = (jnp.concatenate(parts, axis=1) if len(parts) > 1 else parts[0]) + cq
        if mask is not None:
            s = jnp.where(mask, s, NEG_INF)
        m_prev = m_ref[...]
        m_new = jnp.maximum(m_prev, jnp.max(s, axis=1, keepdims=True))
        alpha = jnp.exp(m_prev - m_new)
        p = jnp.exp(s - m_new)
        l_ref[...] = alpha * l_ref[...] + jnp.sum(p, axis=1, keepdims=True)
        m_ref[...] = m_new
        pv = None
        for g, vt in enumerate(vt_list):
            d = _dot_nt(p[:, g * LANES:(g + 1) * LANES].astype(BF16), vt.astype(BF16))
            pv = d if pv is None else pv + d
        acc_ref[...] = alpha * acc_ref[...] + pv

    bias_list = []
    carry = carry_ref[...]
    cum_all = _lane_cumsum(jnp.concatenate([r[0] for r in f_refs], axis=0), u_incl)
    for g in range(pages):
        cum = cum_all[g * FOX_HEADS:(g + 1) * FOX_HEADS, :]
        bias_list.append(-(cum + carry))
        carry = carry + cum[:, LANES - 1:LANES]
    carry_ref[...] = carry
    attend([r[0] for r in k_refs], [r[0] for r in v_refs], bias_list, None)

    @pl.when(j == pl.num_programs(1) - 1)
    def _():
        total = carry_ref[...]
        cn = _lane_cumsum(fn_ref[0], u_incl)
        row = lax.broadcasted_iota(jnp.int32, (rows, LANES), 0)
        lane = lax.broadcasted_iota(jnp.int32, (rows, LANES), 1)
        mask = (lane <= row % n_tok) & (lane < n_tok)
        attend([kn_ref[0]], [vn_ref[0]], [-(cn + total)], mask)
        res = acc_ref[...] / l_ref[...]
        rowh = lax.broadcasted_iota(jnp.int32, (rows, FOX_WIDTH), 0) // n_tok
        colh = lax.broadcasted_iota(jnp.int32, (rows, FOX_WIDTH), 1) // FOX_HEAD_DIM
        res = jnp.where(rowh == colh, res, 0.0)
        y = res[0:n_tok, :]
        for h in range(1, FOX_HEADS):
            y = y + res[h * n_tok:(h + 1) * n_tok, :]
        o_ref[0] = y


def _fox_sample(q, cq, k_new, v_new, logf_new, cache_kt, cache_vt, cache_logf_t, page_table):
    db, n_tok, _ = q.shape
    n_pages = page_table.shape[1]
    page = cache_kt.shape[2]
    assert page == LANES
    pages = math.gcd(DECODE_PAGES_PER_STEP, n_pages)
    rows = FOX_HEADS * n_tok
    h, dh = FOX_HEADS, FOX_HEAD_DIM
    q4 = q.reshape(db, n_tok, h, dh).transpose(0, 2, 1, 3)
    eye = jnp.eye(h, dtype=F32)
    qbd = (q4[:, :, :, None, :] * eye[None, :, None, :, None]).reshape(db, rows, FOX_WIDTH).astype(BF16)
    cq_col = cq.transpose(0, 2, 1).reshape(db, rows, 1)

    def new_page(a):
        return jnp.pad(a.transpose(0, 2, 1), ((0, 0), (0, 0), (0, page - n_tok)))

    kn = new_page(k_new)
    vn = new_page(v_new)
    fn = new_page(logf_new)
    u_incl = (lax.broadcasted_iota(jnp.int32, (page, page), 0)
              <= lax.broadcasted_iota(jnp.int32, (page, page), 1)).astype(BF16)

    def paged(r, g):
        return pl.BlockSpec((1, r, page), lambda b, j, pt: (pt[b * n_pages + j * pages + g], 0, 0))

    def per_batch(r, c):
        return pl.BlockSpec((1, r, c), lambda b, j, pt: (b, 0, 0))

    grid_spec = pltpu.PrefetchScalarGridSpec(
        num_scalar_prefetch=1,
        grid=(db, n_pages // pages),
        in_specs=[per_batch(rows, FOX_WIDTH), per_batch(rows, 1),
                  pl.BlockSpec((page, page), lambda b, j, pt: (0, 0))]
                 + [paged(FOX_WIDTH, g) for g in range(pages)]
                 + [paged(FOX_WIDTH, g) for g in range(pages)]
                 + [paged(h, g) for g in range(pages)]
                 + [per_batch(FOX_WIDTH, page), per_batch(FOX_WIDTH, page), per_batch(h, page)],
        out_specs=per_batch(n_tok, FOX_WIDTH),
        scratch_shapes=[pltpu.VMEM((rows, 1), F32), pltpu.VMEM((rows, 1), F32),
                        pltpu.VMEM((rows, FOX_WIDTH), F32), pltpu.VMEM((h, 1), F32)],
    )
    out = pl.pallas_call(
        functools.partial(_fox_sample_kernel, pages=pages, n_tok=n_tok),
        out_shape=jax.ShapeDtypeStruct((db, n_tok, FOX_WIDTH), F32),
        grid_spec=grid_spec,
        compiler_params=_params("arbitrary", "arbitrary"),
        name="fox_sample",
    )(page_table.reshape(-1), qbd, cq_col, u_incl,
      *([cache_kt] * pages), *([cache_vt] * pages), *([cache_logf_t] * pages), kn, vn, fn)
    return out.reshape(db * n_tok, FOX_WIDTH)


def _gla_spans(lo, hi, blk):
    if hi - lo <= blk:
        return []
    mid = (lo + hi) // 2
    return [(lo, mid, hi)] + _gla_spans(lo, mid, blk) + _gla_spans(mid, hi, blk)


def _gla_kernel(q_ref, k_ref, v_ref, la_ref, r_ref, gain_ref, *rest, chunk, blk, has_s0):
    if has_s0:
        s0_ref, o_ref, sout_ref, st_ref = rest
    else:
        o_ref, sout_ref, st_ref = rest
    c = pl.program_id(1)
    cp = max(chunk, LANES)
    xr = -(-(chunk + 8) // LANES) * LANES

    @pl.when(c == 0)
    def _():
        for hd in range(GLA_HEADS):
            if has_s0:
                st_ref[hd] = s0_ref[0, hd]
            else:
                st_ref[hd] = jnp.zeros((GLA_HK, GLA_HV), F32)

    la = la_ref[...]
    if chunk <= 8:
        b = jnp.zeros_like(la)
        rowi = lax.broadcasted_iota(jnp.int32, (chunk, 1), 0)
        for s in range(chunk):
            b = b + jnp.where(rowi >= s, la[s:s + 1, :], 0.0)
    else:
        ri = lax.broadcasted_iota(jnp.int32, (chunk, chunk), 0)
        ci = lax.broadcasted_iota(jnp.int32, (chunk, chunk), 1)
        b = _dot_exact_lhs01((ci <= ri).astype(BF16), la)

    spans = _gla_spans(0, chunk, blk)
    scale = GLA_HK ** -0.5
    ti = lax.broadcasted_iota(jnp.int32, (blk, blk, 1), 0)
    si = lax.broadcasted_iota(jnp.int32, (blk, blk, 1), 1)
    tri = si <= ti
    gain = gain_ref[...]

    for hd in range(GLA_HEADS):
        ks = slice(hd * GLA_HK, (hd + 1) * GLA_HK)
        vs = slice(hd * GLA_HV, (hd + 1) * GLA_HV)
        bh = b[:, ks]
        qh = q_ref[:, ks].astype(F32) * scale
        kh = k_ref[:, ks].astype(F32)
        vh = v_ref[:, vs].astype(F32)
        b_last = bh[chunk - 1:chunk, :]
        st = st_ref[hd]

        q0 = (qh * jnp.exp(bh)).astype(BF16)
        o = _dot(q0, st.astype(BF16))

        if spans:
            q_parts, k_parts = [], []
            for lo, mid, hi in spans:
                bref = bh[mid - 1:mid, :]
                qg = qh[mid:hi] * jnp.exp(bh[mid:hi] - bref)
                kg = kh[lo:mid] * jnp.exp(bref - bh[lo:mid])
                qz = [jnp.zeros((mid, GLA_HK), F32), qg]
                if chunk > hi:
                    qz.append(jnp.zeros((chunk - hi, GLA_HK), F32))
                kz = ([jnp.zeros((lo, GLA_HK), F32)] if lo > 0 else []) + [kg, jnp.zeros((chunk - mid, GLA_HK), F32)]
                q_parts.append(jnp.concatenate(qz, axis=0))
                k_parts.append(jnp.concatenate(kz, axis=0))
            q_hat = jnp.concatenate(q_parts, axis=1).astype(BF16)
            k_hat = jnp.concatenate(k_parts, axis=1).astype(BF16)
            a_mat = _dot_nt(q_hat, k_hat)
        else:
            a_mat = None
        diag_rows = []
        for i in range(chunk // blk):
            r0 = i * blk
            qi, ki, bi = qh[r0:r0 + blk], kh[r0:r0 + blk], bh[r0:r0 + blk]
            dec = jnp.where(tri, jnp.exp(bi[:, None, :] - bi[None, :, :]), 0.0)
            a_ii = jnp.sum(qi[:, None, :] * ki[None, :, :] * dec, axis=-1)
            if chunk == blk:
                diag_rows.append(a_ii)
            else:
                pieces = ([jnp.zeros((blk, r0), F32)] if r0 > 0 else []) + [a_ii]
                if chunk - r0 - blk > 0:
                    pieces.append(jnp.zeros((blk, chunk - r0 - blk), F32))
                diag_rows.append(jnp.concatenate(pieces, axis=1))
        a_diag = jnp.concatenate(diag_rows, axis=0) if len(diag_rows) > 1 else diag_rows[0]
        a_mat = a_diag if a_mat is None else a_mat + a_diag

        v_pad = vh
        if cp > chunk:
            v_pad = jnp.concatenate([vh, jnp.zeros((cp - chunk, GLA_HV), F32)], axis=0)
            a_mat = jnp.concatenate([a_mat, jnp.zeros((chunk, cp - chunk), F32)], axis=1)
        v_pad = v_pad.astype(BF16)
        o = o + _dot(a_mat.astype(BF16), v_pad)

        k0 = kh * jnp.exp(b_last - bh)
        row8 = lax.broadcasted_iota(jnp.int32, (8, GLA_HK), 0)
        dec_rows = jnp.where(row8 == 0, jnp.broadcast_to(jnp.exp(b_last), (8, GLA_HK)), 0.0)
        x_t = jnp.concatenate([k0, dec_rows, jnp.zeros((xr - chunk - 8, GLA_HK), F32)], axis=0).T
        k0_t = x_t[:, 0:cp].astype(BF16)
        st_ref[hd] = st * x_t[:, chunk:chunk + 1] + _dot(k0_t, v_pad)

        r = r_ref[:, vs].astype(F32)
        o_ref[:, vs] = (_rms(o) * gain * (r * _sigmoid(r))).astype(o_ref.dtype)

    @pl.when(c == pl.num_programs(1) - 1)
    def _():
        for hd in range(GLA_HEADS):
            sout_ref[0, hd] = st_ref[hd]


def _gla(q, k, v, la, r, gain, s0, bsz, t, chunk):
    n_c = t // chunk
    blk = min(GLA_DIAG_BLOCK, chunk)
    has_s0 = s0 is not None

    def tok(cols):
        return pl.BlockSpec((chunk, cols), lambda b, c: (b * n_c + c, 0))

    state_spec = pl.BlockSpec((1, GLA_HEADS, GLA_HK, GLA_HV), lambda b, c: (b, 0, 0, 0))
    in_specs = [tok(GLA_KEY_DIM), tok(GLA_KEY_DIM), tok(GLA_VAL_DIM), tok(GLA_KEY_DIM), tok(GLA_VAL_DIM),
                pl.BlockSpec((1, GLA_HV), lambda b, c: (0, 0))]
    args = [q, k, v, la, r, gain]
    if has_s0:
        in_specs.append(state_spec)
        args.append(s0)
    return pl.pallas_call(
        functools.partial(_gla_kernel, chunk=chunk, blk=blk, has_s0=has_s0),
        out_shape=[jax.ShapeDtypeStruct((bsz * t, GLA_VAL_DIM), BF16),
                   jax.ShapeDtypeStruct((bsz, GLA_HEADS, GLA_HK, GLA_HV), F32)],
        grid=(bsz, n_c),
        in_specs=in_specs,
        out_specs=[tok(GLA_VAL_DIM), state_spec],
        scratch_shapes=[pltpu.VMEM((GLA_HEADS, GLA_HK, GLA_HV), F32)],
        compiler_params=_params("arbitrary", "arbitrary"),
        name="gla",
    )(*args)


def _mix_out_kernel(x_ref, ada_ref, ya_ref, ob_ref, gate_ref, wfox_ref, wgla_ref, wout_ref, g_ref, o_ref):
    tb, tt, d = x_ref.shape
    m = tb * tt
    gates = gate_ref[...].astype(F32)
    merged = (gates[:, 0:d] * _dot(ya_ref[...], wfox_ref[...])
              + gates[:, d:2 * d] * _dot(ob_ref[...], wgla_ref[...]))
    y = _dot(merged.astype(BF16), wout_ref[...])
    y = (_rms(y) * g_ref[...]).reshape(tb, tt, d)
    o_ref[...] = x_ref[...] + ada_ref[:, :, 2 * d:3 * d] * y


def _ffn_kernel(x_ref, ada_ref, gpre_ref, win_ref, wout_ref, gpost_ref, o_ref, *, d_ff, fc):
    tb, tt, d = x_ref.shape
    m = tb * tt
    x = x_ref[...]
    y = _rms(x) * gpre_ref[...]
    h = (y * (1.0 + ada_ref[:, :, 4 * d:5 * d]) + ada_ref[:, :, 3 * d:4 * d]).reshape(m, d).astype(BF16)
    acc = jnp.zeros((m, d), F32)
    for c0 in range(0, d_ff, fc):
        up = _dot(h, win_ref[:, c0:c0 + fc])
        gt = _dot(h, win_ref[:, d_ff + c0:d_ff + c0 + fc])
        acc = acc + _dot((gt * _sigmoid(gt) * up).astype(BF16), wout_ref[c0:c0 + fc, :])
    f = (_rms(acc) * gpost_ref[...]).reshape(tb, tt, d)
    o_ref[...] = x + ada_ref[:, :, 5 * d:6 * d] * f


def _tail(x, ada3, y_a, o_b, gates, wts, *, tb, tt):
    bsz, t, d = x.shape
    m = tb * tt
    n_t = t // tt

    def full(a):
        return pl.BlockSpec(a.shape, lambda i, j: (0,) * a.ndim)

    def tok(cols):
        return pl.BlockSpec((m, cols), lambda i, j: (i * n_t + j, 0))

    x_spec = pl.BlockSpec((tb, tt, d), lambda i, j: (i, j, 0))
    ada_spec = pl.BlockSpec((tb, 1, ada3.shape[2]), lambda i, j: (i, 0, 0))
    consts = (wts["w_out_fox"], wts["w_out_gla"], wts["w_out"], wts["norm_mix_post"])
    x1 = pl.pallas_call(
        _mix_out_kernel,
        out_shape=jax.ShapeDtypeStruct((bsz, t, d), F32),
        grid=(bsz // tb, n_t),
        in_specs=[x_spec, ada_spec, tok(FOX_WIDTH), tok(GLA_VAL_DIM), tok(2 * d)] + [full(a) for a in consts],
        out_specs=x_spec,
        compiler_params=_params("arbitrary", "arbitrary"),
        name="mix_out",
    )(x, ada3, y_a, o_b, gates, *consts)

    d_ff = wts["w_ffn_out"].shape[0]
    fc = 256 if d_ff % 256 == 0 else d_ff
    consts = (wts["norm_ffn_pre"], wts["w_ffn_in"], wts["w_ffn_out"], wts["norm_ffn_post"])
    return pl.pallas_call(
        functools.partial(_ffn_kernel, d_ff=d_ff, fc=fc),
        out_shape=jax.ShapeDtypeStruct((bsz, t, d), F32),
        grid=(bsz // tb, n_t),
        in_specs=[x_spec, ada_spec] + [full(a) for a in consts],
        out_specs=x_spec,
        compiler_params=_params("arbitrary", "arbitrary"),
        name="ffn",
    )(x1, ada3, *consts)


def _prep_weights(layer, norm_mix_pre, norm_mix_post, norm_ffn_pre, norm_ffn_post, w_in, b_forget,
                  w_alpha_up, b_alpha, gla_norm, w_out_fox, w_out_gla, w_out, w_ffn_in, w_ffn_out):
    d = w_in.shape[1]
    w = w_in[layer]
    o_f = 3 * FOX_WIDTH
    o_b = o_f + FOX_HEADS
    o_a = o_b + 2 * GLA_KEY_DIM + 2 * GLA_VAL_DIM
    o_g = o_a + GLA_GATE_RANK
    ws = jnp.zeros((d, 2 * LANES), F32)
    ws = ws.at[:, 0:FOX_HEADS].set(w[:, o_f:o_b])
    ws = ws.at[:, LANES:LANES + GLA_GATE_RANK].set(w[:, o_a:o_g])
    return {
        "wa": w[:, 0:o_f].astype(BF16),
        "ws": ws.astype(BF16),
        "wb": w[:, o_b:o_a].astype(BF16),
        "wg": w[:, o_g:].astype(BF16),
        "bf": jnp.zeros((1, LANES), F32).at[0, 0:FOX_HEADS].set(b_forget[layer]),
        "wup": jnp.zeros((LANES, GLA_KEY_DIM), F32).at[0:GLA_GATE_RANK].set(w_alpha_up[layer]),
        "bal": b_alpha[layer][None, :],
        "norm_mix_pre": norm_mix_pre[layer][None, :],
        "norm_mix_post": norm_mix_post[layer][None, :],
        "norm_ffn_pre": norm_ffn_pre[layer][None, :],
        "norm_ffn_post": norm_ffn_post[layer][None, :],
        "gla_norm": gla_norm[layer][None, :],
        "w_out_fox": w_out_fox[layer].astype(BF16),
        "w_out_gla": w_out_gla[layer].astype(BF16),
        "w_out": w_out[layer].astype(BF16),
        "w_ffn_in": w_ffn_in[layer].astype(BF16),
        "w_ffn_out": w_ffn_out[layer].astype(BF16),
    }


def _tiles(bsz, t):
    if t >= TOKEN_TILE:
        return 1, TOKEN_TILE
    return min(bsz, TOKEN_TILE // t), t


def kernel(x_prompt, x_sample, c_prompt, c_sample, cache_k, cache_v, cache_logf, state_gla, page_table, w_ada, b_ada, norm_mix_pre, norm_mix_post, norm_ffn_pre, norm_ffn_post, w_in, b_forget, w_alpha_up, b_alpha, gla_norm, w_out_fox, w_out_gla, w_out, w_ffn_in, w_ffn_out):
    depth = w_in.shape[0]
    bp, tp, d = x_prompt.shape
    bs, ts, _ = x_sample.shape
    yp, ys = x_prompt, x_sample
    c_all = jnp.concatenate([c_prompt, c_sample], axis=0)
    outs = [[] for _ in range(8)]
    for layer in range(depth):
        wts = _prep_weights(layer, norm_mix_pre, norm_mix_post, norm_ffn_pre, norm_ffn_post, w_in, b_forget,
                            w_alpha_up, b_alpha, gla_norm, w_out_fox, w_out_gla, w_out, w_ffn_in, w_ffn_out)
        ada = _ada_proj(c_all, w_ada[layer], b_ada[layer][None, :])
        ada_p = ada[0:bp].reshape(bp, 1, 6 * d)
        ada_s = ada[bp:].reshape(bs, 1, 6 * d)

        tb, tt = _tiles(bp, tp)
        (qa, kaf, kab, vaf, vab, logf, cum, qb, kb, vb, rb, la, gates) = _in_proj(
            yp, ada_p, wts["norm_mix_pre"], wts, tb=tb, tt=tt, act_dtype=BF16)
        y_a = _fox_prompt(qa, kab, vab, cum, bp, tp)
        o_b, s_new = _gla(qb, kb, vb, la, rb, wts["gla_norm"], None, bp, tp, math.gcd(tp, GLA_PROMPT_CHUNK))
        yp = _tail(yp, ada_p, y_a, o_b, gates, wts, tb=tb, tt=tt)
        def heads_last(a):
            if a.ndim == 3:
                return a.reshape(bp, FOX_HEADS, FOX_HEAD_DIM, tp).transpose(0, 3, 1, 2)
            return a.reshape(bp, tp, FOX_HEADS, FOX_HEAD_DIM)

        outs[0].append(heads_last(kaf))
        outs[1].append(heads_last(vaf))
        outs[2].append(logf.reshape(bp, tp, FOX_HEADS))
        outs[3].append(s_new)

        tb, tt = _tiles(bs, ts)
        (qa, kaf, kab, vaf, vab, logf, cum, qb, kb, vb, rb, la, gates) = _in_proj(
            ys, ada_s, wts["norm_mix_pre"], wts, tb=tb, tt=tt, act_dtype=F32)
        pool, page = cache_k.shape[1], cache_k.shape[2]
        cache_kt = cache_k[layer].transpose(0, 2, 3, 1).reshape(pool, FOX_WIDTH, page)
        cache_vt = cache_v[layer].transpose(0, 2, 3, 1).reshape(pool, FOX_WIDTH, page)
        logf_t = cache_logf[layer].transpose(0, 2, 1)
        y_a = _fox_sample(qa.reshape(bs, ts, FOX_WIDTH), cum.reshape(bs, ts, FOX_HEADS),
                          kaf.reshape(bs, ts, FOX_WIDTH), vaf.reshape(bs, ts, FOX_WIDTH),
                          logf.reshape(bs, ts, FOX_HEADS), cache_kt, cache_vt, logf_t, page_table)
        o_b, s_new = _gla(qb, kb, vb, la, rb, wts["gla_norm"], state_gla[layer], bs, ts, ts)
        ys = _tail(ys, ada_s, y_a.astype(BF16), o_b, gates, wts, tb=tb, tt=tt)
        outs[4].append(kaf.reshape(bs, ts, FOX_HEADS, FOX_HEAD_DIM))
        outs[5].append(vaf.reshape(bs, ts, FOX_HEADS, FOX_HEAD_DIM))
        outs[6].append(logf.reshape(bs, ts, FOX_HEADS))
        outs[7].append(s_new)
    return (yp, ys) + tuple(jnp.stack(o) for o in outs)
```

```python
import functools
import math

import jax
import jax.numpy as jnp
from jax import lax
from jax.experimental import pallas as pl
from jax.experimental.pallas import tpu as pltpu

F32 = jnp.float32
BF16 = jnp.bfloat16

FOX_HEADS = 8
FOX_HEAD_DIM = 64
FOX_WIDTH = FOX_HEADS * FOX_HEAD_DIM
GLA_HEADS = 4
GLA_HK = 128
GLA_HV = 256
GLA_KEY_DIM = GLA_HEADS * GLA_HK
GLA_VAL_DIM = GLA_HEADS * GLA_HV
GLA_GATE_RANK = 16
GLA_GATE_TEMP = 16.0
EPS = 1e-6

LANES = 128
VMEM_LIMIT_BYTES = 56 * 1024 * 1024
TOKEN_TILE = 512
GLA_DIAG_BLOCK = 16
GLA_PROMPT_CHUNK = 128
GLA_SAFE_EXPONENT = 60.0
FOX_Q_TILE = 1024
DECODE_PAGES_PER_STEP = 16
NEG_INF = float("-inf")


def _params(*sem):
    return pltpu.CompilerParams(dimension_semantics=sem, vmem_limit_bytes=VMEM_LIMIT_BYTES)


def _dot(a, b):
    return jnp.dot(a, b, preferred_element_type=F32)


def _dot_nt(a, b):
    return lax.dot_general(a, b, (((1,), (1,)), ((), ())), preferred_element_type=F32)


def _split3(x):
    hi = x.astype(BF16)
    r1 = x - hi.astype(F32)
    mid = r1.astype(BF16)
    lo = (r1 - mid.astype(F32)).astype(BF16)
    return hi, mid, lo


def _dot_exact_lhs01(l01, x):
    hi, mid, lo = _split3(x)
    return _dot(l01, hi) + _dot(l01, mid) + _dot(l01, lo)


def _log_sigmoid(x):
    return jnp.minimum(x, 0.0) - jnp.log1p(jnp.exp(-jnp.abs(x)))


def _sigmoid(x):
    return 1.0 / (1.0 + jnp.exp(-x))


def _rms(x):
    return x * lax.rsqrt(jnp.mean(x * x, axis=-1, keepdims=True) + EPS)


def _ada_kernel(c_ref, w_ref, b_ref, o_ref):
    o_ref[...] = _dot(c_ref[...].astype(BF16), w_ref[...].astype(BF16)) + b_ref[...]


def _ada_proj(c, w, b):
    n, d = c.shape
    cols = w.shape[1]
    bn = 1024
    return pl.pallas_call(
        _ada_kernel,
        out_shape=jax.ShapeDtypeStruct((n, cols), F32),
        grid=(cols // bn,),
        in_specs=[pl.BlockSpec((n, d), lambda j: (0, 0)),
                  pl.BlockSpec((d, bn), lambda j: (0, j)),
                  pl.BlockSpec((1, bn), lambda j: (0, j))],
        out_specs=pl.BlockSpec((n, bn), lambda j: (0, j)),
        compiler_params=_params("arbitrary"),
        name="ada_proj",
    )(c, w, b)


def _in_proj_kernel(x_ref, ada_ref, g_ref, wa_ref, ws_ref, wb_ref, wg_ref, bf_ref, wup_ref,
                    bal_ref, lmat_ref,
                    qa_ref, kaf_ref, kab_ref, vaf_ref, vab_ref, logf_ref, cum_ref,
                    qb_ref, kb_ref, vb_ref, rb_ref, la_ref, gate_ref, carry_ref,
                    *, carry_rows, kv_transposed):
    tb, tt, d = x_ref.shape
    m = tb * tt
    x = x_ref[...]
    y = _rms(x) * g_ref[...]
    shift = ada_ref[:, :, 0:d]
    scale = ada_ref[:, :, d:2 * d]
    h = (y * (1.0 + scale) + shift).reshape(m, d).astype(BF16)

    za = _dot(h, wa_ref[...])
    w = FOX_WIDTH
    qa_ref[...] = (za[:, 0:w] * (FOX_HEAD_DIM ** -0.5)).astype(qa_ref.dtype)
    ka = za[:, w:2 * w]
    va = za[:, 2 * w:3 * w]
    if kv_transposed:
        kaf_ref[0] = ka.T
        vaf_ref[0] = va.T
    else:
        kaf_ref[...] = ka
        vaf_ref[...] = va
    kab_ref[...] = ka.astype(BF16)
    vab_ref[...] = va.astype(BF16)

    zs = _dot(h, ws_ref[...])
    logf = _log_sigmoid(zs[:, 0:LANES] + bf_ref[...])
    logf_ref[...] = logf[:, 0:FOX_HEADS]
    cum = _dot_exact_lhs01(lmat_ref[...], logf)
    if carry_rows:
        @pl.when(pl.program_id(1) == 0)
        def _():
            carry_ref[...] = jnp.zeros_like(carry_ref)
        cum = cum + carry_ref[0:1, :]
        carry_ref[0:1, :] = cum[m - 1:m, :]
    cum_ref[...] = cum[:, 0:FOX_HEADS]

    a_low = zs[:, LANES:2 * LANES]
    a_hi = a_low.astype(BF16)
    a_mid = (a_low - a_hi.astype(F32)).astype(BF16)
    wup = wup_ref[...]
    w_hi = wup.astype(BF16)
    w_mid = (wup - w_hi.astype(F32)).astype(BF16)
    a_logit = _dot(a_hi, w_hi) + _dot(a_hi, w_mid) + _dot(a_mid, w_hi) + bal_ref[...]
    la_ref[...] = _log_sigmoid(a_logit) * (1.0 / GLA_GATE_TEMP)

    zb = _dot(h, wb_ref[...])
    kd, vd = GLA_KEY_DIM, GLA_VAL_DIM
    qb_ref[...] = zb[:, 0:kd].astype(qb_ref.dtype)
    kb_ref[...] = zb[:, kd:2 * kd].astype(kb_ref.dtype)
    vb_ref[...] = zb[:, 2 * kd:2 * kd + vd].astype(vb_ref.dtype)
    rb_ref[...] = zb[:, 2 * kd + vd:2 * kd + 2 * vd].astype(rb_ref.dtype)

    gate_ref[...] = _sigmoid(_dot(h, wg_ref[...])).astype(gate_ref.dtype)


def _in_proj(x, ada3, gain, wts, *, tb, tt, act_dtype):
    bsz, t, d = x.shape
    m = tb * tt
    n_t = t // tt
    n_tok = bsz * t
    carry_rows = n_t > 1
    assert (tb == 1) or (tt == t)
    kv_transposed = tb == 1 and tt % LANES == 0
    row = lax.broadcasted_iota(jnp.int32, (m, m), 0)
    col = lax.broadcasted_iota(jnp.int32, (m, m), 1)
    lmat = ((col <= row) & (row // tt == col // tt)).astype(BF16)

    def full(a):
        return pl.BlockSpec(a.shape, lambda i, j: (0,) * a.ndim)

    def tok(cols):
        return pl.BlockSpec((m, cols), lambda i, j: (i * n_t + j, 0))

    out_cols = [(FOX_WIDTH, act_dtype), (FOX_WIDTH, F32), (FOX_WIDTH, BF16), (FOX_WIDTH, F32),
                (FOX_WIDTH, BF16), (FOX_HEADS, F32), (FOX_HEADS, F32),
                (GLA_KEY_DIM, act_dtype), (GLA_KEY_DIM, act_dtype), (GLA_VAL_DIM, act_dtype),
                (GLA_VAL_DIM, act_dtype), (GLA_KEY_DIM, F32), (2 * d, BF16)]
    consts = (gain, wts["wa"], wts["ws"], wts["wb"], wts["wg"], wts["bf"], wts["wup"], wts["bal"], lmat)
    out_shape = [jax.ShapeDtypeStruct((n_tok, c), dt) for c, dt in out_cols]
    out_specs = [tok(c) for c, _ in out_cols]
    if kv_transposed:
        for idx in (1, 3):
            out_shape[idx] = jax.ShapeDtypeStruct((bsz, FOX_WIDTH, t), F32)
            out_specs[idx] = pl.BlockSpec((1, FOX_WIDTH, tt), lambda i, j: (i, 0, j))
    return pl.pallas_call(
        functools.partial(_in_proj_kernel, carry_rows=carry_rows, kv_transposed=kv_transposed),
        out_shape=out_shape,
        grid=(bsz // tb, n_t),
        in_specs=[pl.BlockSpec((tb, tt, d), lambda i, j: (i, j, 0)),
                  pl.BlockSpec((tb, 1, ada3.shape[2]), lambda i, j: (i, 0, 0))]
                 + [full(a) for a in consts],
        out_specs=out_specs,
        scratch_shapes=[pltpu.VMEM((8, LANES), F32)],
        compiler_params=_params("arbitrary", "arbitrary"),
        name="in_proj",
    )(x, ada3, *consts)


def _fox_prompt_kernel(q_ref, k_ref, v_ref, cq_ref, ck_ref, o_ref, *, tq, tk):
    qi = pl.program_id(2)
    q = q_ref[...]
    lane = lax.broadcasted_iota(jnp.int32, (1, LANES), 1)
    lo_half = lane < FOX_HEAD_DIM
    q_heads = (jnp.where(lo_half, q, jnp.zeros_like(q)), jnp.where(lo_half, jnp.zeros_like(q), q))
    cq = cq_ref[0, 0]
    cq_heads = (cq[:, 0:1], cq[:, 1:2])
    row = lax.broadcasted_iota(jnp.int32, (tq, tk), 0)
    col = lax.broadcasted_iota(jnp.int32, (tq, tk), 1)

    def block(ki, carry, masked):
        k0 = pl.multiple_of(ki * tk, tk)
        k = k_ref[pl.ds(k0, tk), :]
        v = v_ref[pl.ds(k0, tk), :]
        ck = ck_ref[0, 0, :, pl.ds(k0, tk)]
        out = []
        for g in range(2):
            m_prev, l_prev, acc_prev = carry[g]
            s = _dot_nt(q_heads[g], k) + cq_heads[g] - ck[g:g + 1, :]
            if masked:
                s = jnp.where(col <= row, s, NEG_INF)
            m_new = jnp.maximum(m_prev, jnp.max(s, axis=1, keepdims=True))
            alpha = jnp.exp(m_prev - m_new)
            p = jnp.exp(s - m_new)
            l_new = alpha * l_prev + jnp.sum(p, axis=1, keepdims=True)
            acc_new = alpha * acc_prev + _dot(p.astype(BF16), v)
            out.append((m_new, l_new, acc_new))
        return tuple(out)

    init = tuple((jnp.full((tq, 1), NEG_INF, F32), jnp.zeros((tq, 1), F32),
                  jnp.zeros((tq, LANES), F32)) for _ in range(2))
    carry = lax.fori_loop(0, qi, lambda ki, c: block(ki, c, False), init)
    carry = block(qi, carry, True)
    (m0, l0, a0), (m1, l1, a1) = carry
    o_ref[...] = jnp.where(lo_half, a0 / l0, a1 / l1).astype(o_ref.dtype)


def _fox_prompt(q, k, v, cum, bsz, t):
    tq = min(FOX_Q_TILE, t)
    tk = tq
    n_q = t // tq
    pairs = FOX_HEADS // 2
    cum4 = cum.reshape(bsz, t, pairs, 2)
    cq = cum4.transpose(0, 2, 1, 3)
    ck = cum4.transpose(0, 2, 3, 1)
    return pl.pallas_call(
        functools.partial(_fox_prompt_kernel, tq=tq, tk=tk),
        out_shape=jax.ShapeDtypeStruct((bsz * t, FOX_WIDTH), BF16),
        grid=(bsz, pairs, n_q),
        in_specs=[pl.BlockSpec((tq, LANES), lambda b, hp, i: (b * n_q + i, hp)),
                  pl.BlockSpec((t, LANES), lambda b, hp, i: (b, hp)),
                  pl.BlockSpec((t, LANES), lambda b, hp, i: (b, hp)),
                  pl.BlockSpec((1, 1, tq, 2), lambda b, hp, i: (b, hp, i, 0)),
                  pl.BlockSpec((1, 1, 2, t), lambda b, hp, i: (b, hp, 0, 0))],
        out_specs=pl.BlockSpec((tq, LANES), lambda b, hp, i: (b * n_q + i, hp)),
        compiler_params=_params("arbitrary", "arbitrary", "arbitrary"),
        name="fox_prompt",
    )(q, k, v, cq, ck)


def _lane_cumsum(x, u_incl):
    hi, mid, lo = _split3(x)
    return _dot(hi, u_incl) + _dot(mid, u_incl) + _dot(lo, u_incl)


def _fox_sample_kernel(pt_ref, q_ref, cq_ref, u_ref, *refs, pages, n_tok):
    k_refs = refs[0:pages]
    v_refs = refs[pages:2 * pages]
    f_refs = refs[2 * pages:3 * pages]
    kn_ref, vn_ref, fn_ref, o_ref, m_ref, l_ref, acc_ref, carry_ref = refs[3 * pages:]
    u_incl = u_ref[...]
    j = pl.program_id(1)
    rows = FOX_HEADS * n_tok

    @pl.when(j == 0)
    def _():
        m_ref[...] = jnp.full_like(m_ref, NEG_INF)
        l_ref[...] = jnp.zeros_like(l_ref)
        acc_ref[...] = jnp.zeros_like(acc_ref)
        carry_ref[...] = jnp.zeros_like(carry_ref)

    qbd = q_ref[0]
    cq = cq_ref[0]

    def per_row(x):
        return jnp.concatenate([jnp.broadcast_to(x[h:h + 1, :], (n_tok, LANES)) for h in range(FOX_HEADS)], axis=0)

    def attend(kt_list, vt_list, bias_list, mask):
        parts = [_dot(qbd, kt.astype(BF16)) + per_row(bias) for kt, bias in zip(kt_list, bias_list)]
        s = (jnp.concatenate(parts, axis=1) if len(parts) > 1 else parts[0]) + cq
        if mask is not None:
            s = jnp.where(mask, s, NEG_INF)
        m_prev = m_ref[...]
        m_new = jnp.maximum(m_prev, jnp.max(s, axis=1, keepdims=True))
        alpha = jnp.exp(m_prev - m_new)
        p = jnp.exp(s - m_new)
        l_ref[...] = alpha * l_ref[...] + jnp.sum(p, axis=1, keepdims=True)
        m_ref[...] = m_new
        pv = None
        for g, vt in enumerate(vt_list):
            d = _dot_nt(p[:, g * LANES:(g + 1) * LANES].astype(BF16), vt.astype(BF16))
            pv = d if pv is None else pv + d
        acc_ref[...] = alpha * acc_ref[...] + pv

    bias_list = []
    carry = carry_ref[...]
    cum_all = _lane_cumsum(jnp.concatenate([r[0] for r in f_refs], axis=0), u_incl)
    for g in range(pages):
        cum = cum_all[g * FOX_HEADS:(g + 1) * FOX_HEADS, :]
        bias_list.append(-(cum + carry))
        carry = carry + cum[:, LANES - 1:LANES]
    carry_ref[...] = carry
    attend([r[0] for r in k_refs], [r[0] for r in v_refs], bias_list, None)

    @pl.when(j == pl.num_programs(1) - 1)
    def _():
        total = carry_ref[...]
        cn = _lane_cumsum(fn_ref[0], u_incl)
        row = lax.broadcasted_iota(jnp.int32, (rows, LANES), 0)
        lane = lax.broadcasted_iota(jnp.int32, (rows, LANES), 1)
        mask = (lane <= row % n_tok) & (lane < n_tok)
        attend([kn_ref[0]], [vn_ref[0]], [-(cn + total)], mask)
        res = acc_ref[...] / l_ref[...]
        rowh = lax.broadcasted_iota(jnp.int32, (rows, FOX_WIDTH), 0) // n_tok
        colh = lax.broadcasted_iota(jnp.int32, (rows, FOX_WIDTH), 1) // FOX_HEAD_DIM
        res = jnp.where(rowh == colh, res, 0.0)
        y = res[0:n_tok, :]
        for h in range(1, FOX_HEADS):
            y = y + res[h * n_tok:(h + 1) * n_tok, :]
        o_ref[0] = y


def _fox_sample(q, cq, k_new, v_new, logf_new, cache_kt, cache_vt, cache_logf_t, page_table):
    db, n_tok, _ = q.shape
    n_pages = page_table.shape[1]
    page = cache_kt.shape[2]
    assert page == LANES
    pages = math.gcd(DECODE_PAGES_PER_STEP, n_pages)
    rows = FOX_HEADS * n_tok
    h, dh = FOX_HEADS, FOX_HEAD_DIM
    q4 = q.reshape(db, n_tok, h, dh).transpose(0, 2, 1, 3)
    eye = jnp.eye(h, dtype=F32)
    qbd = (q4[:, :, :, None, :] * eye[None, :, None, :, None]).reshape(db, rows, FOX_WIDTH).astype(BF16)
    cq_col = cq.transpose(0, 2, 1).reshape(db, rows, 1)

    def new_page(a):
        return jnp.pad(a.transpose(0, 2, 1), ((0, 0), (0, 0), (0, page - n_tok)))

    kn = new_page(k_new)
    vn = new_page(v_new)
    fn = new_page(logf_new)
    u_incl = (lax.broadcasted_iota(jnp.int32, (page, page), 0)
              <= lax.broadcasted_iota(jnp.int32, (page, page), 1)).astype(BF16)

    def paged(r, g):
        return pl.BlockSpec((1, r, page), lambda b, j, pt: (pt[b * n_pages + j * pages + g], 0, 0))

    def per_batch(r, c):
        return pl.BlockSpec((1, r, c), lambda b, j, pt: (b, 0, 0))

    grid_spec = pltpu.PrefetchScalarGridSpec(
        num_scalar_prefetch=1,
        grid=(db, n_pages // pages),
        in_specs=[per_batch(rows, FOX_WIDTH), per_batch(rows, 1),
                  pl.BlockSpec((page, page), lambda b, j, pt: (0, 0))]
                 + [paged(FOX_WIDTH, g) for g in range(pages)]
                 + [paged(FOX_WIDTH, g) for g in range(pages)]
                 + [paged(h, g) for g in range(pages)]
                 + [per_batch(FOX_WIDTH, page), per_batch(FOX_WIDTH, page), per_batch(h, page)],
        out_specs=per_batch(n_tok, FOX_WIDTH),
        scratch_shapes=[pltpu.VMEM((rows, 1), F32), pltpu.VMEM((rows, 1), F32),
                        pltpu.VMEM((rows, FOX_WIDTH), F32), pltpu.VMEM((h, 1), F32)],
    )
    out = pl.pallas_call(
        functools.partial(_fox_sample_kernel, pages=pages, n_tok=n_tok),
        out_shape=jax.ShapeDtypeStruct((db, n_tok, FOX_WIDTH), F32),
        grid_spec=grid_spec,
        compiler_params=_params("arbitrary", "arbitrary"),
        name="fox_sample",
    )(page_table.reshape(-1), qbd, cq_col, u_incl,
      *([cache_kt] * pages), *([cache_vt] * pages), *([cache_logf_t] * pages), kn, vn, fn)
    return out.reshape(db * n_tok, FOX_WIDTH)


def _gla_spans(lo, hi, blk):
    if hi - lo <= blk:
        return []
    mid = (lo + hi) // 2
    return [(lo, mid, hi)] + _gla_spans(lo, mid, blk) + _gla_spans(mid, hi, blk)


def _gla_kernel(q_ref, k_ref, v_ref, la_ref, r_ref, gain_ref, *rest, chunk, blk, has_s0):
    if has_s0:
        s0_ref, o_ref, sout_ref, st_ref = rest
    else:
        o_ref, sout_ref, st_ref = rest
    c = pl.program_id(1)
    cp = max(chunk, LANES)
    xr = -(-(chunk + 8) // LANES) * LANES

    @pl.when(c == 0)
    def _():
        for hd in range(GLA_HEADS):
            if has_s0:
                st_ref[hd] = s0_ref[0, hd]
            else:
                st_ref[hd] = jnp.zeros((GLA_HK, GLA_HV), F32)

    la = la_ref[...]
    if chunk <= 8:
        b = jnp.zeros_like(la)
        rowi = lax.broadcasted_iota(jnp.int32, (chunk, 1), 0)
        for s in range(chunk):
            b = b + jnp.where(rowi >= s, la[s:s + 1, :], 0.0)
    else:
        ri = lax.broadcasted_iota(jnp.int32, (chunk, chunk), 0)
        ci = lax.broadcasted_iota(jnp.int32, (chunk, chunk), 1)
        b = _dot_exact_lhs01((ci <= ri).astype(BF16), la)

    spans = _gla_spans(0, chunk, blk)
    scale = GLA_HK ** -0.5
    ti = lax.broadcasted_iota(jnp.int32, (blk, blk, 1), 0)
    si = lax.broadcasted_iota(jnp.int32, (blk, blk, 1), 1)
    tri = si <= ti
    gain = gain_ref[...]
    if chunk > blk:
        span = None
        for r0 in range(0, chunk, blk):
            d = b[r0:r0 + 1, :] - b[r0 + blk - 1:r0 + blk, :]
            span = d if span is None else jnp.maximum(span, d)
        blocks_safe = jnp.max(span) <= GLA_SAFE_EXPONENT
        rr = lax.broadcasted_iota(jnp.int32, (chunk, chunk), 0)
        cc = lax.broadcasted_iota(jnp.int32, (chunk, chunk), 1)
        blockdiag_causal = (rr // blk == cc // blk) & (cc <= rr)

    def one_head(hd, factored):
        ks = slice(hd * GLA_HK, (hd + 1) * GLA_HK)
        vs = slice(hd * GLA_HV, (hd + 1) * GLA_HV)
        bh = b[:, ks]
        qh = q_ref[:, ks].astype(F32) * scale
        kh = k_ref[:, ks].astype(F32)
        vh = v_ref[:, vs].astype(F32)
        b_last = bh[chunk - 1:chunk, :]
        st = st_ref[hd]

        q0 = (qh * jnp.exp(bh)).astype(BF16)
        o = _dot(q0, st.astype(BF16))

        if spans:
            q_parts, k_parts = [], []
            for lo, mid, hi in spans:
                bref = bh[mid - 1:mid, :]
                qg = qh[mid:hi] * jnp.exp(bh[mid:hi] - bref)
                kg = kh[lo:mid] * jnp.exp(bref - bh[lo:mid])
                qz = [jnp.zeros((mid, GLA_HK), F32), qg]
                if chunk > hi:
                    qz.append(jnp.zeros((chunk - hi, GLA_HK), F32))
                kz = ([jnp.zeros((lo, GLA_HK), F32)] if lo > 0 else []) + [kg, jnp.zeros((chunk - mid, GLA_HK), F32)]
                q_parts.append(jnp.concatenate(qz, axis=0))
                k_parts.append(jnp.concatenate(kz, axis=0))
            q_hat = jnp.concatenate(q_parts, axis=1).astype(BF16)
            k_hat = jnp.concatenate(k_parts, axis=1).astype(BF16)
            a_mat = _dot_nt(q_hat, k_hat)
        else:
            a_mat = None
        def diag_exact(qh=qh, kh=kh, bh=bh):
            diag_rows = []
            for i in range(chunk // blk):
                r0 = i * blk
                qi, ki, bi = qh[r0:r0 + blk], kh[r0:r0 + blk], bh[r0:r0 + blk]
                dec = jnp.where(tri, jnp.exp(bi[:, None, :] - bi[None, :, :]), 0.0)
                a_ii = jnp.sum(qi[:, None, :] * ki[None, :, :] * dec, axis=-1)
                if chunk == blk:
                    diag_rows.append(a_ii)
                else:
                    pieces = ([jnp.zeros((blk, r0), F32)] if r0 > 0 else []) + [a_ii]
                    if chunk - r0 - blk > 0:
                        pieces.append(jnp.zeros((blk, chunk - r0 - blk), F32))
                    diag_rows.append(jnp.concatenate(pieces, axis=1))
            return jnp.concatenate(diag_rows, axis=0) if len(diag_rows) > 1 else diag_rows[0]

        def diag_factored(qh=qh, kh=kh, bh=bh):
            bref = jnp.concatenate([jnp.broadcast_to(bh[r0:r0 + 1, :], (blk, GLA_HK))
                                    for r0 in range(0, chunk, blk)], axis=0)
            qd = (qh * jnp.exp(bh - bref)).astype(BF16)
            kd = (kh * jnp.exp(bref - bh)).astype(BF16)
            return jnp.where(blockdiag_causal, _dot_nt(qd, kd), 0.0)

        a_diag = diag_factored() if factored else diag_exact()
        a_mat = a_diag if a_mat is None else a_mat + a_diag

        v_pad = vh
        if cp > chunk:
            v_pad = jnp.concatenate([vh, jnp.zeros((cp - chunk, GLA_HV), F32)], axis=0)
            a_mat = jnp.concatenate([a_mat, jnp.zeros((chunk, cp - chunk), F32)], axis=1)
        v_pad = v_pad.astype(BF16)
        o = o + _dot(a_mat.astype(BF16), v_pad)

        k0 = kh * jnp.exp(b_last - bh)
        row8 = lax.broadcasted_iota(jnp.int32, (8, GLA_HK), 0)
        dec_rows = jnp.where(row8 == 0, jnp.broadcast_to(jnp.exp(b_last), (8, GLA_HK)), 0.0)
        x_t = jnp.concatenate([k0, dec_rows, jnp.zeros((xr - chunk - 8, GLA_HK), F32)], axis=0).T
        k0_t = x_t[:, 0:cp].astype(BF16)
        st_ref[hd] = st * x_t[:, chunk:chunk + 1] + _dot(k0_t, v_pad)

        r = r_ref[:, vs].astype(F32)
        o_ref[:, vs] = (_rms(o) * gain * (r * _sigmoid(r))).astype(o_ref.dtype)

    if chunk > blk:
        @pl.when(blocks_safe)
        def _():
            for hd in range(GLA_HEADS):
                one_head(hd, True)

        @pl.when(jnp.logical_not(blocks_safe))
        def _():
            for hd in range(GLA_HEADS):
                one_head(hd, False)
    else:
        for hd in range(GLA_HEADS):
            one_head(hd, False)

    @pl.when(c == pl.num_programs(1) - 1)
    def _():
        for hd in range(GLA_HEADS):
            sout_ref[0, hd] = st_ref[hd]


def _gla(q, k, v, la, r, gain, s0, bsz, t, chunk):
    n_c = t // chunk
    blk = min(GLA_DIAG_BLOCK, chunk)
    has_s0 = s0 is not None

    def tok(cols):
        return pl.BlockSpec((chunk, cols), lambda b, c: (b * n_c + c, 0))

    state_spec = pl.BlockSpec((1, GLA_HEADS, GLA_HK, GLA_HV), lambda b, c: (b, 0, 0, 0))
    in_specs = [tok(GLA_KEY_DIM), tok(GLA_KEY_DIM), tok(GLA_VAL_DIM), tok(GLA_KEY_DIM), tok(GLA_VAL_DIM),
                pl.BlockSpec((1, GLA_HV), lambda b, c: (0, 0))]
    args = [q, k, v, la, r, gain]
    if has_s0:
        in_specs.append(state_spec)
        args.append(s0)
    return pl.pallas_call(
        functools.partial(_gla_kernel, chunk=chunk, blk=blk, has_s0=has_s0),
        out_shape=[jax.ShapeDtypeStruct((bsz * t, GLA_VAL_DIM), BF16),
                   jax.ShapeDtypeStruct((bsz, GLA_HEADS, GLA_HK, GLA_HV), F32)],
        grid=(bsz, n_c),
        in_specs=in_specs,
        out_specs=[tok(GLA_VAL_DIM), state_spec],
        scratch_shapes=[pltpu.VMEM((GLA_HEADS, GLA_HK, GLA_HV), F32)],
        compiler_params=_params("arbitrary", "arbitrary"),
        name="gla",
    )(*args)


def _mix_out_kernel(x_ref, ada_ref, ya_ref, ob_ref, gate_ref, wfox_ref, wgla_ref, wout_ref, g_ref, o_ref):
    tb, tt, d = x_ref.shape
    m = tb * tt
    gates = gate_ref[...].astype(F32)
    merged = (gates[:, 0:d] * _dot(ya_ref[...], wfox_ref[...])
              + gates[:, d:2 * d] * _dot(ob_ref[...], wgla_ref[...]))
    y = _dot(merged.astype(BF16), wout_ref[...])
    y = (_rms(y) * g_ref[...]).reshape(tb, tt, d)
    o_ref[...] = x_ref[...] + ada_ref[:, :, 2 * d:3 * d] * y


def _ffn_kernel(x_ref, ada_ref, gpre_ref, win_ref, wout_ref, gpost_ref, o_ref, *, d_ff, fc):
    tb, tt, d = x_ref.shape
    m = tb * tt
    x = x_ref[...]
    y = _rms(x) * gpre_ref[...]
    h = (y * (1.0 + ada_ref[:, :, 4 * d:5 * d]) + ada_ref[:, :, 3 * d:4 * d]).reshape(m, d).astype(BF16)
    acc = jnp.zeros((m, d), F32)
    for c0 in range(0, d_ff, fc):
        up = _dot(h, win_ref[:, c0:c0 + fc])
        gt = _dot(h, win_ref[:, d_ff + c0:d_ff + c0 + fc])
        acc = acc + _dot((gt * _sigmoid(gt) * up).astype(BF16), wout_ref[c0:c0 + fc, :])
    f = (_rms(acc) * gpost_ref[...]).reshape(tb, tt, d)
    o_ref[...] = x + ada_ref[:, :, 5 * d:6 * d] * f


def _tail(x, ada3, y_a, o_b, gates, wts, *, tb, tt):
    bsz, t, d = x.shape
    m = tb * tt
    n_t = t // tt

    def full(a):
        return pl.BlockSpec(a.shape, lambda i, j: (0,) * a.ndim)

    def tok(cols):
        return pl.BlockSpec((m, cols), lambda i, j: (i * n_t + j, 0))

    x_spec = pl.BlockSpec((tb, tt, d), lambda i, j: (i, j, 0))
    ada_spec = pl.BlockSpec((tb, 1, ada3.shape[2]), lambda i, j: (i, 0, 0))
    consts = (wts["w_out_fox"], wts["w_out_gla"], wts["w_out"], wts["norm_mix_post"])
    x1 = pl.pallas_call(
        _mix_out_kernel,
        out_shape=jax.ShapeDtypeStruct((bsz, t, d), F32),
        grid=(bsz // tb, n_t),
        in_specs=[x_spec, ada_spec, tok(FOX_WIDTH), tok(GLA_VAL_DIM), tok(2 * d)] + [full(a) for a in consts],
        out_specs=x_spec,
        compiler_params=_params("arbitrary", "arbitrary"),
        name="mix_out",
    )(x, ada3, y_a, o_b, gates, *consts)

    d_ff = wts["w_ffn_out"].shape[0]
    fc = 256 if d_ff % 256 == 0 else d_ff
    consts = (wts["norm_ffn_pre"], wts["w_ffn_in"], wts["w_ffn_out"], wts["norm_ffn_post"])
    return pl.pallas_call(
        functools.partial(_ffn_kernel, d_ff=d_ff, fc=fc),
        out_shape=jax.ShapeDtypeStruct((bsz, t, d), F32),
        grid=(bsz // tb, n_t),
        in_specs=[x_spec, ada_spec] + [full(a) for a in consts],
        out_specs=x_spec,
        compiler_params=_params("arbitrary", "arbitrary"),
        name="ffn",
    )(x1, ada3, *consts)


def _prep_weights(layer, norm_mix_pre, norm_mix_post, norm_ffn_pre, norm_ffn_post, w_in, b_forget,
                  w_alpha_up, b_alpha, gla_norm, w_out_fox, w_out_gla, w_out, w_ffn_in, w_ffn_out):
    d = w_in.shape[1]
    w = w_in[layer]
    o_f = 3 * FOX_WIDTH
    o_b = o_f + FOX_HEADS
    o_a = o_b + 2 * GLA_KEY_DIM + 2 * GLA_VAL_DIM
    o_g = o_a + GLA_GATE_RANK
    ws = jnp.zeros((d, 2 * LANES), F32)
    ws = ws.at[:, 0:FOX_HEADS].set(w[:, o_f:o_b])
    ws = ws.at[:, LANES:LANES + GLA_GATE_RANK].set(w[:, o_a:o_g])
    return {
        "wa": w[:, 0:o_f].astype(BF16),
        "ws": ws.astype(BF16),
        "wb": w[:, o_b:o_a].astype(BF16),
        "wg": w[:, o_g:].astype(BF16),
        "bf": jnp.zeros((1, LANES), F32).at[0, 0:FOX_HEADS].set(b_forget[layer]),
        "wup": jnp.zeros((LANES, GLA_KEY_DIM), F32).at[0:GLA_GATE_RANK].set(w_alpha_up[layer]),
        "bal": b_alpha[layer][None, :],
        "norm_mix_pre": norm_mix_pre[layer][None, :],
        "norm_mix_post": norm_mix_post[layer][None, :],
        "norm_ffn_pre": norm_ffn_pre[layer][None, :],
        "norm_ffn_post": norm_ffn_post[layer][None, :],
        "gla_norm": gla_norm[layer][None, :],
        "w_out_fox": w_out_fox[layer].astype(BF16),
        "w_out_gla": w_out_gla[layer].astype(BF16),
        "w_out": w_out[layer].astype(BF16),
        "w_ffn_in": w_ffn_in[layer].astype(BF16),
        "w_ffn_out": w_ffn_out[layer].astype(BF16),
    }


def _tiles(bsz, t):
    if t >= TOKEN_TILE:
        return 1, TOKEN_TILE
    return min(bsz, TOKEN_TILE // t), t


def kernel(x_prompt, x_sample, c_prompt, c_sample, cache_k, cache_v, cache_logf, state_gla, page_table, w_ada, b_ada, norm_mix_pre, norm_mix_post, norm_ffn_pre, norm_ffn_post, w_in, b_forget, w_alpha_up, b_alpha, gla_norm, w_out_fox, w_out_gla, w_out, w_ffn_in, w_ffn_out):
    depth = w_in.shape[0]
    bp, tp, d = x_prompt.shape
    bs, ts, _ = x_sample.shape
    yp, ys = x_prompt, x_sample
    c_all = jnp.concatenate([c_prompt, c_sample], axis=0)
    outs = [[] for _ in range(8)]
    for layer in range(depth):
        wts = _prep_weights(layer, norm_mix_pre, norm_mix_post, norm_ffn_pre, norm_ffn_post, w_in, b_forget,
                            w_alpha_up, b_alpha, gla_norm, w_out_fox, w_out_gla, w_out, w_ffn_in, w_ffn_out)
        ada = _ada_proj(c_all, w_ada[layer], b_ada[layer][None, :])
        ada_p = ada[0:bp].reshape(bp, 1, 6 * d)
        ada_s = ada[bp:].reshape(bs, 1, 6 * d)

        tb, tt = _tiles(bp, tp)
        (qa, kaf, kab, vaf, vab, logf, cum, qb, kb, vb, rb, la, gates) = _in_proj(
            yp, ada_p, wts["norm_mix_pre"], wts, tb=tb, tt=tt, act_dtype=BF16)
        y_a = _fox_prompt(qa, kab, vab, cum, bp, tp)
        o_b, s_new = _gla(qb, kb, vb, la, rb, wts["gla_norm"], None, bp, tp, math.gcd(tp, GLA_PROMPT_CHUNK))
        yp = _tail(yp, ada_p, y_a, o_b, gates, wts, tb=tb, tt=tt)
        def heads_last(a):
            if a.ndim == 3:
                return a.reshape(bp, FOX_HEADS, FOX_HEAD_DIM, tp).transpose(0, 3, 1, 2)
            return a.reshape(bp, tp, FOX_HEADS, FOX_HEAD_DIM)

        outs[0].append(heads_last(kaf))
        outs[1].append(heads_last(vaf))
        outs[2].append(logf.reshape(bp, tp, FOX_HEADS))
        outs[3].append(s_new)

        tb, tt = _tiles(bs, ts)
        (qa, kaf, kab, vaf, vab, logf, cum, qb, kb, vb, rb, la, gates) = _in_proj(
            ys, ada_s, wts["norm_mix_pre"], wts, tb=tb, tt=tt, act_dtype=F32)
        pool, page = cache_k.shape[1], cache_k.shape[2]
        cache_kt = cache_k[layer].transpose(0, 2, 3, 1).reshape(pool, FOX_WIDTH, page)
        cache_vt = cache_v[layer].transpose(0, 2, 3, 1).reshape(pool, FOX_WIDTH, page)
        logf_t = cache_logf[layer].transpose(0, 2, 1)
        y_a = _fox_sample(qa.reshape(bs, ts, FOX_WIDTH), cum.reshape(bs, ts, FOX_HEADS),
                          kaf.reshape(bs, ts, FOX_WIDTH), vaf.reshape(bs, ts, FOX_WIDTH),
                          logf.reshape(bs, ts, FOX_HEADS), cache_kt, cache_vt, logf_t, page_table)
        o_b, s_new = _gla(qb, kb, vb, la, rb, wts["gla_norm"], state_gla[layer], bs, ts, ts)
        ys = _tail(ys, ada_s, y_a.astype(BF16), o_b, gates, wts, tb=tb, tt=tt)
        outs[4].append(kaf.reshape(bs, ts, FOX_HEADS, FOX_HEAD_DIM))
        outs[5].append(vaf.reshape(bs, ts, FOX_HEADS, FOX_HEAD_DIM))
        outs[6].append(logf.reshape(bs, ts, FOX_HEADS))
        outs[7].append(s_new)
    return (yp, ys) + tuple(jnp.stack(o) for o in outs)
```

```python
import functools
import math

import jax
import jax.numpy as jnp
from jax import lax
from jax.experimental import pallas as pl
from jax.experimental.pallas import tpu as pltpu

F32 = jnp.float32
BF16 = jnp.bfloat16

FOX_HEADS = 8
FOX_HEAD_DIM = 64
FOX_WIDTH = FOX_HEADS * FOX_HEAD_DIM
GLA_HEADS = 4
GLA_HK = 128
GLA_HV = 256
GLA_KEY_DIM = GLA_HEADS * GLA_HK
GLA_VAL_DIM = GLA_HEADS * GLA_HV
GLA_GATE_RANK = 16
GLA_GATE_TEMP = 16.0
EPS = 1e-6

LANES = 128
VMEM_LIMIT_BYTES = 56 * 1024 * 1024
TOKEN_TILE = 512
GLA_DIAG_BLOCK = 16
GLA_PROMPT_CHUNK = 128
GLA_SAFE_EXPONENT = 60.0
FOX_Q_TILE = 1024
DECODE_PAGES_PER_STEP = 32
NEG_INF = float("-inf")


def _params(*sem):
    return pltpu.CompilerParams(dimension_semantics=sem, vmem_limit_bytes=VMEM_LIMIT_BYTES)


def _dot(a, b):
    return jnp.dot(a, b, preferred_element_type=F32)


def _dot_nt(a, b):
    return lax.dot_general(a, b, (((1,), (1,)), ((), ())), preferred_element_type=F32)


def _split3(x):
    hi = x.astype(BF16)
    r1 = x - hi.astype(F32)
    mid = r1.astype(BF16)
    lo = (r1 - mid.astype(F32)).astype(BF16)
    return hi, mid, lo


def _dot_exact_lhs01(l01, x):
    hi, mid, lo = _split3(x)
    return _dot(l01, hi) + _dot(l01, mid) + _dot(l01, lo)


def _log_sigmoid(x):
    return jnp.minimum(x, 0.0) - jnp.log1p(jnp.exp(-jnp.abs(x)))


def _sigmoid(x):
    return 1.0 / (1.0 + jnp.exp(-x))


def _rms(x):
    return x * lax.rsqrt(jnp.mean(x * x, axis=-1, keepdims=True) + EPS)


def _ada_kernel(c_ref, w_ref, b_ref, o_ref):
    o_ref[...] = _dot(c_ref[...].astype(BF16), w_ref[...].astype(BF16)) + b_ref[...]


def _ada_proj(c, w, b):
    n, d = c.shape
    cols = w.shape[1]
    bn = 1024
    return pl.pallas_call(
        _ada_kernel,
        out_shape=jax.ShapeDtypeStruct((n, cols), F32),
        grid=(cols // bn,),
        in_specs=[pl.BlockSpec((n, d), lambda j: (0, 0)),
                  pl.BlockSpec((d, bn), lambda j: (0, j)),
                  pl.BlockSpec((1, bn), lambda j: (0, j))],
        out_specs=pl.BlockSpec((n, bn), lambda j: (0, j)),
        compiler_params=_params("arbitrary"),
        name="ada_proj",
    )(c, w, b)


def _in_proj_kernel(x_ref, ada_ref, g_ref, wa_ref, ws_ref, wb_ref, wg_ref, bf_ref, wup_ref,
                    bal_ref, lmat_ref,
                    qa_ref, kaf_ref, kab_ref, vaf_ref, vab_ref, logf_ref, cum_ref,
                    qb_ref, kb_ref, vb_ref, rb_ref, la_ref, gate_ref, carry_ref,
                    *, carry_rows, kv_transposed):
    tb, tt, d = x_ref.shape
    m = tb * tt
    x = x_ref[...]
    y = _rms(x) * g_ref[...]
    shift = ada_ref[:, :, 0:d]
    scale = ada_ref[:, :, d:2 * d]
    h = (y * (1.0 + scale) + shift).reshape(m, d).astype(BF16)

    za = _dot(h, wa_ref[...])
    w = FOX_WIDTH
    qa_ref[...] = (za[:, 0:w] * (FOX_HEAD_DIM ** -0.5)).astype(qa_ref.dtype)
    ka = za[:, w:2 * w]
    va = za[:, 2 * w:3 * w]
    if kv_transposed:
        kaf_ref[0] = ka.T
        vaf_ref[0] = va.T
    else:
        kaf_ref[...] = ka
        vaf_ref[...] = va
    kab_ref[...] = ka.astype(BF16)
    vab_ref[...] = va.astype(BF16)

    zs = _dot(h, ws_ref[...])
    logf = _log_sigmoid(zs[:, 0:LANES] + bf_ref[...])
    logf_ref[...] = logf[:, 0:FOX_HEADS]
    cum = _dot_exact_lhs01(lmat_ref[...], logf)
    if carry_rows:
        @pl.when(pl.program_id(1) == 0)
        def _():
            carry_ref[...] = jnp.zeros_like(carry_ref)
        cum = cum + carry_ref[0:1, :]
        carry_ref[0:1, :] = cum[m - 1:m, :]
    cum_ref[...] = cum[:, 0:FOX_HEADS]

    a_low = zs[:, LANES:2 * LANES]
    a_hi = a_low.astype(BF16)
    a_mid = (a_low - a_hi.astype(F32)).astype(BF16)
    wup = wup_ref[...]
    w_hi = wup.astype(BF16)
    w_mid = (wup - w_hi.astype(F32)).astype(BF16)
    a_logit = _dot(a_hi, w_hi) + _dot(a_hi, w_mid) + _dot(a_mid, w_hi) + bal_ref[...]
    la_ref[...] = _log_sigmoid(a_logit) * (1.0 / GLA_GATE_TEMP)

    zb = _dot(h, wb_ref[...])
    kd, vd = GLA_KEY_DIM, GLA_VAL_DIM
    qb_ref[...] = zb[:, 0:kd].astype(qb_ref.dtype)
    kb_ref[...] = zb[:, kd:2 * kd].astype(kb_ref.dtype)
    vb_ref[...] = zb[:, 2 * kd:2 * kd + vd].astype(vb_ref.dtype)
    rb_ref[...] = zb[:, 2 * kd + vd:2 * kd + 2 * vd].astype(rb_ref.dtype)

    gate_ref[...] = _sigmoid(_dot(h, wg_ref[...])).astype(gate_ref.dtype)


def _in_proj(x, ada3, gain, wts, *, tb, tt, act_dtype):
    bsz, t, d = x.shape
    m = tb * tt
    n_t = t // tt
    n_tok = bsz * t
    carry_rows = n_t > 1
    assert (tb == 1) or (tt == t)
    kv_transposed = tb == 1 and tt % LANES == 0
    row = lax.broadcasted_iota(jnp.int32, (m, m), 0)
    col = lax.broadcasted_iota(jnp.int32, (m, m), 1)
    lmat = ((col <= row) & (row // tt == col // tt)).astype(BF16)

    def full(a):
        return pl.BlockSpec(a.shape, lambda i, j: (0,) * a.ndim)

    def tok(cols):
        return pl.BlockSpec((m, cols), lambda i, j: (i * n_t + j, 0))

    out_cols = [(FOX_WIDTH, act_dtype), (FOX_WIDTH, F32), (FOX_WIDTH, BF16), (FOX_WIDTH, F32),
                (FOX_WIDTH, BF16), (FOX_HEADS, F32), (FOX_HEADS, F32),
                (GLA_KEY_DIM, act_dtype), (GLA_KEY_DIM, act_dtype), (GLA_VAL_DIM, act_dtype),
                (GLA_VAL_DIM, act_dtype), (GLA_KEY_DIM, F32), (2 * d, BF16)]
    consts = (gain, wts["wa"], wts["ws"], wts["wb"], wts["wg"], wts["bf"], wts["wup"], wts["bal"], lmat)
    out_shape = [jax.ShapeDtypeStruct((n_tok, c), dt) for c, dt in out_cols]
    out_specs = [tok(c) for c, _ in out_cols]
    if kv_transposed:
        for idx in (1, 3):
            out_shape[idx] = jax.ShapeDtypeStruct((bsz, FOX_WIDTH, t), F32)
            out_specs[idx] = pl.BlockSpec((1, FOX_WIDTH, tt), lambda i, j: (i, 0, j))
    return pl.pallas_call(
        functools.partial(_in_proj_kernel, carry_rows=carry_rows, kv_transposed=kv_transposed),
        out_shape=out_shape,
        grid=(bsz // tb, n_t),
        in_specs=[pl.BlockSpec((tb, tt, d), lambda i, j: (i, j, 0)),
                  pl.BlockSpec((tb, 1, ada3.shape[2]), lambda i, j: (i, 0, 0))]
                 + [full(a) for a in consts],
        out_specs=out_specs,
        scratch_shapes=[pltpu.VMEM((8, LANES), F32)],
        compiler_params=_params("arbitrary", "arbitrary"),
        name="in_proj",
    )(x, ada3, *consts)


def _fox_prompt_kernel(q_ref, k_ref, v_ref, cq_ref, ck_ref, o_ref, *, tq, tk):
    qi = pl.program_id(2)
    q = q_ref[...]
    lane = lax.broadcasted_iota(jnp.int32, (1, LANES), 1)
    lo_half = lane < FOX_HEAD_DIM
    q_heads = (jnp.where(lo_half, q, jnp.zeros_like(q)), jnp.where(lo_half, jnp.zeros_like(q), q))
    cq = cq_ref[0, 0]
    cq_heads = (cq[:, 0:1], cq[:, 1:2])
    row = lax.broadcasted_iota(jnp.int32, (tq, tk), 0)
    col = lax.broadcasted_iota(jnp.int32, (tq, tk), 1)

    def block(ki, carry, masked):
        k0 = pl.multiple_of(ki * tk, tk)
        k = k_ref[pl.ds(k0, tk), :]
        v = v_ref[pl.ds(k0, tk), :]
        ck = ck_ref[0, 0, :, pl.ds(k0, tk)]
        out = []
        for g in range(2):
            m_prev, l_prev, acc_prev = carry[g]
            s = _dot_nt(q_heads[g], k) + cq_heads[g] - ck[g:g + 1, :]
            if masked:
                s = jnp.where(col <= row, s, NEG_INF)
            m_new = jnp.maximum(m_prev, jnp.max(s, axis=1, keepdims=True))
            alpha = jnp.exp(m_prev - m_new)
            p = jnp.exp(s - m_new)
            l_new = alpha * l_prev + jnp.sum(p, axis=1, keepdims=True)
            acc_new = alpha * acc_prev + _dot(p.astype(BF16), v)
            out.append((m_new, l_new, acc_new))
        return tuple(out)

    init = tuple((jnp.full((tq, 1), NEG_INF, F32), jnp.zeros((tq, 1), F32),
                  jnp.zeros((tq, LANES), F32)) for _ in range(2))
    carry = lax.fori_loop(0, qi, lambda ki, c: block(ki, c, False), init)
    carry = block(qi, carry, True)
    (m0, l0, a0), (m1, l1, a1) = carry
    o_ref[...] = jnp.where(lo_half, a0 / l0, a1 / l1).astype(o_ref.dtype)


def _fox_prompt(q, k, v, cum, bsz, t):
    tq = min(FOX_Q_TILE, t)
    tk = tq
    n_q = t // tq
    pairs = FOX_HEADS // 2
    cum4 = cum.reshape(bsz, t, pairs, 2)
    cq = cum4.transpose(0, 2, 1, 3)
    ck = cum4.transpose(0, 2, 3, 1)
    return pl.pallas_call(
        functools.partial(_fox_prompt_kernel, tq=tq, tk=tk),
        out_shape=jax.ShapeDtypeStruct((bsz * t, FOX_WIDTH), BF16),
        grid=(bsz, pairs, n_q),
        in_specs=[pl.BlockSpec((tq, LANES), lambda b, hp, i: (b * n_q + i, hp)),
                  pl.BlockSpec((t, LANES), lambda b, hp, i: (b, hp)),
                  pl.BlockSpec((t, LANES), lambda b, hp, i: (b, hp)),
                  pl.BlockSpec((1, 1, tq, 2), lambda b, hp, i: (b, hp, i, 0)),
                  pl.BlockSpec((1, 1, 2, t), lambda b, hp, i: (b, hp, 0, 0))],
        out_specs=pl.BlockSpec((tq, LANES), lambda b, hp, i: (b * n_q + i, hp)),
        compiler_params=_params("arbitrary", "arbitrary", "arbitrary"),
        name="fox_prompt",
    )(q, k, v, cq, ck)


def _lane_cumsum(x, u_incl):
    hi, mid, lo = _split3(x)
    return _dot(hi, u_incl) + _dot(mid, u_incl) + _dot(lo, u_incl)


def _fox_sample_kernel(pt_ref, q_ref, cq_ref, u_ref, *refs, pages, n_tok):
    k_refs = refs[0:pages]
    v_refs = refs[pages:2 * pages]
    f_refs = refs[2 * pages:3 * pages]
    kn_ref, vn_ref, fn_ref, o_ref, m_ref, l_ref, acc_ref, carry_ref = refs[3 * pages:]
    u_incl = u_ref[...]
    j = pl.program_id(1)
    rows = FOX_HEADS * n_tok

    @pl.when(j == 0)
    def _():
        m_ref[...] = jnp.full_like(m_ref, NEG_INF)
        l_ref[...] = jnp.zeros_like(l_ref)
        acc_ref[...] = jnp.zeros_like(acc_ref)
        carry_ref[...] = jnp.zeros_like(carry_ref)

    qbd = q_ref[0]
    cq = cq_ref[0]

    def per_row(x):
        return jnp.concatenate([jnp.broadcast_to(x[h:h + 1, :], (n_tok, LANES)) for h in range(FOX_HEADS)], axis=0)

    def attend(kt_list, vt_list, bias_list, mask):
        parts = [_dot(qbd, kt.astype(BF16)) + per_row(bias) for kt, bias in zip(kt_list, bias_list)]
        s = (jnp.concatenate(parts, axis=1) if len(parts) > 1 else parts[0]) + cq
        if mask is not None:
            s = jnp.where(mask, s, NEG_INF)
        m_prev = m_ref[...]
        m_new = jnp.maximum(m_prev, jnp.max(s, axis=1, keepdims=True))
        alpha = jnp.exp(m_prev - m_new)
        p = jnp.exp(s - m_new)
        l_ref[...] = alpha * l_ref[...] + jnp.sum(p, axis=1, keepdims=True)
        m_ref[...] = m_new
        pv = None
        for g, vt in enumerate(vt_list):
            d = _dot_nt(p[:, g * LANES:(g + 1) * LANES].astype(BF16), vt.astype(BF16))
            pv = d if pv is None else pv + d
        acc_ref[...] = alpha * acc_ref[...] + pv

    bias_list = []
    carry = carry_ref[...]
    cum_all = _lane_cumsum(jnp.concatenate([r[0] for r in f_refs], axis=0), u_incl)
    for g in range(pages):
        cum = cum_all[g * FOX_HEADS:(g + 1) * FOX_HEADS, :]
        bias_list.append(-(cum + carry))
        carry = carry + cum[:, LANES - 1:LANES]
    carry_ref[...] = carry
    attend([r[0] for r in k_refs], [r[0] for r in v_refs], bias_list, None)

    @pl.when(j == pl.num_programs(1) - 1)
    def _():
        total = carry_ref[...]
        cn = _lane_cumsum(fn_ref[0], u_incl)
        row = lax.broadcasted_iota(jnp.int32, (rows, LANES), 0)
        lane = lax.broadcasted_iota(jnp.int32, (rows, LANES), 1)
        mask = (lane <= row % n_tok) & (lane < n_tok)
        attend([kn_ref[0]], [vn_ref[0]], [-(cn + total)], mask)
        res = acc_ref[...] / l_ref[...]
        rowh = lax.broadcasted_iota(jnp.int32, (rows, FOX_WIDTH), 0) // n_tok
        colh = lax.broadcasted_iota(jnp.int32, (rows, FOX_WIDTH), 1) // FOX_HEAD_DIM
        res = jnp.where(rowh == colh, res, 0.0)
        y = res[0:n_tok, :]
        for h in range(1, FOX_HEADS):
            y = y + res[h * n_tok:(h + 1) * n_tok, :]
        o_ref[0] = y


def _fox_sample(q, cq, k_new, v_new, logf_new, cache_kt, cache_vt, cache_logf_t, page_table):
    db, n_tok, _ = q.shape
    n_pages = page_table.shape[1]
    page = cache_kt.shape[2]
    assert page == LANES
    pages = math.gcd(DECODE_PAGES_PER_STEP, n_pages)
    rows = FOX_HEADS * n_tok
    h, dh = FOX_HEADS, FOX_HEAD_DIM
    q4 = q.reshape(db, n_tok, h, dh).transpose(0, 2, 1, 3)
    eye = jnp.eye(h, dtype=F32)
    qbd = (q4[:, :, :, None, :] * eye[None, :, None, :, None]).reshape(db, rows, FOX_WIDTH).astype(BF16)
    cq_col = cq.transpose(0, 2, 1).reshape(db, rows, 1)

    def new_page(a):
        return jnp.pad(a.transpose(0, 2, 1), ((0, 0), (0, 0), (0, page - n_tok)))

    kn = new_page(k_new)
    vn = new_page(v_new)
    fn = new_page(logf_new)
    u_incl = (lax.broadcasted_iota(jnp.int32, (page, page), 0)
              <= lax.broadcasted_iota(jnp.int32, (page, page), 1)).astype(BF16)

    def paged(r, g):
        return pl.BlockSpec((1, r, page), lambda b, j, pt: (pt[b * n_pages + j * pages + g], 0, 0))

    def per_batch(r, c):
        return pl.BlockSpec((1, r, c), lambda b, j, pt: (b, 0, 0))

    grid_spec = pltpu.PrefetchScalarGridSpec(
        num_scalar_prefetch=1,
        grid=(db, n_pages // pages),
        in_specs=[per_batch(rows, FOX_WIDTH), per_batch(rows, 1),
                  pl.BlockSpec((page, page), lambda b, j, pt: (0, 0))]
                 + [paged(FOX_WIDTH, g) for g in range(pages)]
                 + [paged(FOX_WIDTH, g) for g in range(pages)]
                 + [paged(h, g) for g in range(pages)]
                 + [per_batch(FOX_WIDTH, page), per_batch(FOX_WIDTH, page), per_batch(h, page)],
        out_specs=per_batch(n_tok, FOX_WIDTH),
        scratch_shapes=[pltpu.VMEM((rows, 1), F32), pltpu.VMEM((rows, 1), F32),
                        pltpu.VMEM((rows, FOX_WIDTH), F32), pltpu.VMEM((h, 1), F32)],
    )
    out = pl.pallas_call(
        functools.partial(_fox_sample_kernel, pages=pages, n_tok=n_tok),
        out_shape=jax.ShapeDtypeStruct((db, n_tok, FOX_WIDTH), F32),
        grid_spec=grid_spec,
        compiler_params=_params("arbitrary", "arbitrary"),
        name="fox_sample",
    )(page_table.reshape(-1), qbd, cq_col, u_incl,
      *([cache_kt] * pages), *([cache_vt] * pages), *([cache_logf_t] * pages), kn, vn, fn)
    return out.reshape(db * n_tok, FOX_WIDTH)


def _gla_spans(lo, hi, blk):
    if hi - lo <= blk:
        return []
    mid = (lo + hi) // 2
    return [(lo, mid, hi)] + _gla_spans(lo, mid, blk) + _gla_spans(mid, hi, blk)


def _gla_kernel(q_ref, k_ref, v_ref, la_ref, r_ref, gain_ref, *rest, chunk, blk, has_s0):
    if has_s0:
        s0_ref, o_ref, sout_ref, st_ref = rest
    else:
        o_ref, sout_ref, st_ref = rest
    c = pl.program_id(1)
    cp = max(chunk, LANES)
    xr = -(-(chunk + 8) // LANES) * LANES

    @pl.when(c == 0)
    def _():
        for hd in range(GLA_HEADS):
            if has_s0:
                st_ref[hd] = s0_ref[0, hd]
            else:
                st_ref[hd] = jnp.zeros((GLA_HK, GLA_HV), F32)

    la = la_ref[...]
    if chunk <= 8:
        b = jnp.zeros_like(la)
        rowi = lax.broadcasted_iota(jnp.int32, (chunk, 1), 0)
        for s in range(chunk):
            b = b + jnp.where(rowi >= s, la[s:s + 1, :], 0.0)
    else:
        ri = lax.broadcasted_iota(jnp.int32, (chunk, chunk), 0)
        ci = lax.broadcasted_iota(jnp.int32, (chunk, chunk), 1)
        b = _dot_exact_lhs01((ci <= ri).astype(BF16), la)

    spans = _gla_spans(0, chunk, blk)
    scale = GLA_HK ** -0.5
    ti = lax.broadcasted_iota(jnp.int32, (blk, blk, 1), 0)
    si = lax.broadcasted_iota(jnp.int32, (blk, blk, 1), 1)
    tri = si <= ti
    gain = gain_ref[...]
    if chunk > blk:
        span = None
        for r0 in range(0, chunk, blk):
            d = b[r0:r0 + 1, :] - b[r0 + blk - 1:r0 + blk, :]
            span = d if span is None else jnp.maximum(span, d)
        blocks_safe = jnp.max(span) <= GLA_SAFE_EXPONENT
        rr = lax.broadcasted_iota(jnp.int32, (chunk, chunk), 0)
        cc = lax.broadcasted_iota(jnp.int32, (chunk, chunk), 1)
        blockdiag_causal = (rr // blk == cc // blk) & (cc <= rr)

    def one_head(hd, factored):
        ks = slice(hd * GLA_HK, (hd + 1) * GLA_HK)
        vs = slice(hd * GLA_HV, (hd + 1) * GLA_HV)
        bh = b[:, ks]
        qh = q_ref[:, ks].astype(F32) * scale
        kh = k_ref[:, ks].astype(F32)
        vh = v_ref[:, vs].astype(F32)
        b_last = bh[chunk - 1:chunk, :]
        st = st_ref[hd]

        q0 = (qh * jnp.exp(bh)).astype(BF16)
        o = _dot(q0, st.astype(BF16))

        if spans:
            q_parts, k_parts = [], []
            for lo, mid, hi in spans:
                bref = bh[mid - 1:mid, :]
                qg = qh[mid:hi] * jnp.exp(bh[mid:hi] - bref)
                kg = kh[lo:mid] * jnp.exp(bref - bh[lo:mid])
                qz = [jnp.zeros((mid, GLA_HK), F32), qg]
                if chunk > hi:
                    qz.append(jnp.zeros((chunk - hi, GLA_HK), F32))
                kz = ([jnp.zeros((lo, GLA_HK), F32)] if lo > 0 else []) + [kg, jnp.zeros((chunk - mid, GLA_HK), F32)]
                q_parts.append(jnp.concatenate(qz, axis=0))
                k_parts.append(jnp.concatenate(kz, axis=0))
            q_hat = jnp.concatenate(q_parts, axis=1).astype(BF16)
            k_hat = jnp.concatenate(k_parts, axis=1).astype(BF16)
            a_mat = _dot_nt(q_hat, k_hat)
        else:
            a_mat = None
        def diag_exact(qh=qh, kh=kh, bh=bh):
            diag_rows = []
            for i in range(chunk // blk):
                r0 = i * blk
                qi, ki, bi = qh[r0:r0 + blk], kh[r0:r0 + blk], bh[r0:r0 + blk]
                dec = jnp.where(tri, jnp.exp(bi[:, None, :] - bi[None, :, :]), 0.0)
                a_ii = jnp.sum(qi[:, None, :] * ki[None, :, :] * dec, axis=-1)
                if chunk == blk:
                    diag_rows.append(a_ii)
                else:
                    pieces = ([jnp.zeros((blk, r0), F32)] if r0 > 0 else []) + [a_ii]
                    if chunk - r0 - blk > 0:
                        pieces.append(jnp.zeros((blk, chunk - r0 - blk), F32))
                    diag_rows.append(jnp.concatenate(pieces, axis=1))
            return jnp.concatenate(diag_rows, axis=0) if len(diag_rows) > 1 else diag_rows[0]

        def diag_factored(qh=qh, kh=kh, bh=bh):
            bref = jnp.concatenate([jnp.broadcast_to(bh[r0:r0 + 1, :], (blk, GLA_HK))
                                    for r0 in range(0, chunk, blk)], axis=0)
            qd = (qh * jnp.exp(bh - bref)).astype(BF16)
            kd = (kh * jnp.exp(bref - bh)).astype(BF16)
            return jnp.where(blockdiag_causal, _dot_nt(qd, kd), 0.0)

        a_diag = diag_factored() if factored else diag_exact()
        a_mat = a_diag if a_mat is None else a_mat + a_diag

        v_pad = vh
        if cp > chunk:
            v_pad = jnp.concatenate([vh, jnp.zeros((cp - chunk, GLA_HV), F32)], axis=0)
            a_mat = jnp.concatenate([a_mat, jnp.zeros((chunk, cp - chunk), F32)], axis=1)
        v_pad = v_pad.astype(BF16)
        o = o + _dot(a_mat.astype(BF16), v_pad)

        k0 = kh * jnp.exp(b_last - bh)
        row8 = lax.broadcasted_iota(jnp.int32, (8, GLA_HK), 0)
        dec_rows = jnp.where(row8 == 0, jnp.broadcast_to(jnp.exp(b_last), (8, GLA_HK)), 0.0)
        x_t = jnp.concatenate([k0, dec_rows, jnp.zeros((xr - chunk - 8, GLA_HK), F32)], axis=0).T
        k0_t = x_t[:, 0:cp].astype(BF16)
        st_ref[hd] = st * x_t[:, chunk:chunk + 1] + _dot(k0_t, v_pad)

        r = r_ref[:, vs].astype(F32)
        o_ref[:, vs] = (_rms(o) * gain * (r * _sigmoid(r))).astype(o_ref.dtype)

    if chunk > blk:
        @pl.when(blocks_safe)
        def _():
            for hd in range(GLA_HEADS):
                one_head(hd, True)

        @pl.when(jnp.logical_not(blocks_safe))
        def _():
            for hd in range(GLA_HEADS):
                one_head(hd, False)
    else:
        for hd in range(GLA_HEADS):
            one_head(hd, False)

    @pl.when(c == pl.num_programs(1) - 1)
    def _():
        for hd in range(GLA_HEADS):
            sout_ref[0, hd] = st_ref[hd]


def _gla(q, k, v, la, r, gain, s0, bsz, t, chunk):
    n_c = t // chunk
    blk = min(GLA_DIAG_BLOCK, chunk)
    has_s0 = s0 is not None

    def tok(cols):
        return pl.BlockSpec((chunk, cols), lambda b, c: (b * n_c + c, 0))

    state_spec = pl.BlockSpec((1, GLA_HEADS, GLA_HK, GLA_HV), lambda b, c: (b, 0, 0, 0))
    in_specs = [tok(GLA_KEY_DIM), tok(GLA_KEY_DIM), tok(GLA_VAL_DIM), tok(GLA_KEY_DIM), tok(GLA_VAL_DIM),
                pl.BlockSpec((1, GLA_HV), lambda b, c: (0, 0))]
    args = [q, k, v, la, r, gain]
    if has_s0:
        in_specs.append(state_spec)
        args.append(s0)
    return pl.pallas_call(
        functools.partial(_gla_kernel, chunk=chunk, blk=blk, has_s0=has_s0),
        out_shape=[jax.ShapeDtypeStruct((bsz * t, GLA_VAL_DIM), BF16),
                   jax.ShapeDtypeStruct((bsz, GLA_HEADS, GLA_HK, GLA_HV), F32)],
        grid=(bsz, n_c),
        in_specs=in_specs,
        out_specs=[tok(GLA_VAL_DIM), state_spec],
        scratch_shapes=[pltpu.VMEM((GLA_HEADS, GLA_HK, GLA_HV), F32)],
        compiler_params=_params("arbitrary", "arbitrary"),
        name="gla",
    )(*args)


def _mix_out_kernel(x_ref, ada_ref, ya_ref, ob_ref, gate_ref, wfox_ref, wgla_ref, wout_ref, g_ref, o_ref):
    tb, tt, d = x_ref.shape
    m = tb * tt
    gates = gate_ref[...].astype(F32)
    merged = (gates[:, 0:d] * _dot(ya_ref[...], wfox_ref[...])
              + gates[:, d:2 * d] * _dot(ob_ref[...], wgla_ref[...]))
    y = _dot(merged.astype(BF16), wout_ref[...])
    y = (_rms(y) * g_ref[...]).reshape(tb, tt, d)
    o_ref[...] = x_ref[...] + ada_ref[:, :, 2 * d:3 * d] * y


def _ffn_kernel(x_ref, ada_ref, gpre_ref, win_ref, wout_ref, gpost_ref, o_ref, *, d_ff, fc):
    tb, tt, d = x_ref.shape
    m = tb * tt
    x = x_ref[...]
    y = _rms(x) * gpre_ref[...]
    h = (y * (1.0 + ada_ref[:, :, 4 * d:5 * d]) + ada_ref[:, :, 3 * d:4 * d]).reshape(m, d).astype(BF16)
    acc = jnp.zeros((m, d), F32)
    for c0 in range(0, d_ff, fc):
        up = _dot(h, win_ref[:, c0:c0 + fc])
        gt = _dot(h, win_ref[:, d_ff + c0:d_ff + c0 + fc])
        acc = acc + _dot((gt * _sigmoid(gt) * up).astype(BF16), wout_ref[c0:c0 + fc, :])
    f = (_rms(acc) * gpost_ref[...]).reshape(tb, tt, d)
    o_ref[...] = x + ada_ref[:, :, 5 * d:6 * d] * f


def _tail(x, ada3, y_a, o_b, gates, wts, *, tb, tt):
    bsz, t, d = x.shape
    m = tb * tt
    n_t = t // tt

    def full(a):
        return pl.BlockSpec(a.shape, lambda i, j: (0,) * a.ndim)

    def tok(cols):
        return pl.BlockSpec((m, cols), lambda i, j: (i * n_t + j, 0))

    x_spec = pl.BlockSpec((tb, tt, d), lambda i, j: (i, j, 0))
    ada_spec = pl.BlockSpec((tb, 1, ada3.shape[2]), lambda i, j: (i, 0, 0))
    consts = (wts["w_out_fox"], wts["w_out_gla"], wts["w_out"], wts["norm_mix_post"])
    x1 = pl.pallas_call(
        _mix_out_kernel,
        out_shape=jax.ShapeDtypeStruct((bsz, t, d), F32),
        grid=(bsz // tb, n_t),
        in_specs=[x_spec, ada_spec, tok(FOX_WIDTH), tok(GLA_VAL_DIM), tok(2 * d)] + [full(a) for a in consts],
        out_specs=x_spec,
        compiler_params=_params("arbitrary", "arbitrary"),
        name="mix_out",
    )(x, ada3, y_a, o_b, gates, *consts)

    d_ff = wts["w_ffn_out"].shape[0]
    fc = 256 if d_ff % 256 == 0 else d_ff
    consts = (wts["norm_ffn_pre"], wts["w_ffn_in"], wts["w_ffn_out"], wts["norm_ffn_post"])
    return pl.pallas_call(
        functools.partial(_ffn_kernel, d_ff=d_ff, fc=fc),
        out_shape=jax.ShapeDtypeStruct((bsz, t, d), F32),
        grid=(bsz // tb, n_t),
        in_specs=[x_spec, ada_spec] + [full(a) for a in consts],
        out_specs=x_spec,
        compiler_params=_params("arbitrary", "arbitrary"),
        name="ffn",
    )(x1, ada3, *consts)


def _prep_weights(layer, norm_mix_pre, norm_mix_post, norm_ffn_pre, norm_ffn_post, w_in, b_forget,
                  w_alpha_up, b_alpha, gla_norm, w_out_fox, w_out_gla, w_out, w_ffn_in, w_ffn_out):
    d = w_in.shape[1]
    w = w_in[layer]
    o_f = 3 * FOX_WIDTH
    o_b = o_f + FOX_HEADS
    o_a = o_b + 2 * GLA_KEY_DIM + 2 * GLA_VAL_DIM
    o_g = o_a + GLA_GATE_RANK
    ws = jnp.zeros((d, 2 * LANES), F32)
    ws = ws.at[:, 0:FOX_HEADS].set(w[:, o_f:o_b])
    ws = ws.at[:, LANES:LANES + GLA_GATE_RANK].set(w[:, o_a:o_g])
    return {
        "wa": w[:, 0:o_f].astype(BF16),
        "ws": ws.astype(BF16),
        "wb": w[:, o_b:o_a].astype(BF16),
        "wg": w[:, o_g:].astype(BF16),
        "bf": jnp.zeros((1, LANES), F32).at[0, 0:FOX_HEADS].set(b_forget[layer]),
        "wup": jnp.zeros((LANES, GLA_KEY_DIM), F32).at[0:GLA_GATE_RANK].set(w_alpha_up[layer]),
        "bal": b_alpha[layer][None, :],
        "norm_mix_pre": norm_mix_pre[layer][None, :],
        "norm_mix_post": norm_mix_post[layer][None, :],
        "norm_ffn_pre": norm_ffn_pre[layer][None, :],
        "norm_ffn_post": norm_ffn_post[layer][None, :],
        "gla_norm": gla_norm[layer][None, :],
        "w_out_fox": w_out_fox[layer].astype(BF16),
        "w_out_gla": w_out_gla[layer].astype(BF16),
        "w_out": w_out[layer].astype(BF16),
        "w_ffn_in": w_ffn_in[layer].astype(BF16),
        "w_ffn_out": w_ffn_out[layer].astype(BF16),
    }


def _tiles(bsz, t):
    if t >= TOKEN_TILE:
        return 1, TOKEN_TILE
    return min(bsz, TOKEN_TILE // t), t


def kernel(x_prompt, x_sample, c_prompt, c_sample, cache_k, cache_v, cache_logf, state_gla, page_table, w_ada, b_ada, norm_mix_pre, norm_mix_post, norm_ffn_pre, norm_ffn_post, w_in, b_forget, w_alpha_up, b_alpha, gla_norm, w_out_fox, w_out_gla, w_out, w_ffn_in, w_ffn_out):
    depth = w_in.shape[0]
    bp, tp, d = x_prompt.shape
    bs, ts, _ = x_sample.shape
    yp, ys = x_prompt, x_sample
    c_all = jnp.concatenate([c_prompt, c_sample], axis=0)
    outs = [[] for _ in range(8)]
    for layer in range(depth):
        wts = _prep_weights(layer, norm_mix_pre, norm_mix_post, norm_ffn_pre, norm_ffn_post, w_in, b_forget,
                            w_alpha_up, b_alpha, gla_norm, w_out_fox, w_out_gla, w_out, w_ffn_in, w_ffn_out)
        ada = _ada_proj(c_all, w_ada[layer], b_ada[layer][None, :])
        ada_p = ada[0:bp].reshape(bp, 1, 6 * d)
        ada_s = ada[bp:].reshape(bs, 1, 6 * d)

        tb, tt = _tiles(bp, tp)
        (qa, kaf, kab, vaf, vab, logf, cum, qb, kb, vb, rb, la, gates) = _in_proj(
            yp, ada_p, wts["norm_mix_pre"], wts, tb=tb, tt=tt, act_dtype=BF16)
        y_a = _fox_prompt(qa, kab, vab, cum, bp, tp)
        o_b, s_new = _gla(qb, kb, vb, la, rb, wts["gla_norm"], None, bp, tp, math.gcd(tp, GLA_PROMPT_CHUNK))
        yp = _tail(yp, ada_p, y_a, o_b, gates, wts, tb=tb, tt=tt)
        def heads_last(a):
            if a.ndim == 3:
                return a.reshape(bp, FOX_HEADS, FOX_HEAD_DIM, tp).transpose(0, 3, 1, 2)
            return a.reshape(bp, tp, FOX_HEADS, FOX_HEAD_DIM)

        outs[0].append(heads_last(kaf))
        outs[1].append(heads_last(vaf))
        outs[2].append(logf.reshape(bp, tp, FOX_HEADS))
        outs[3].append(s_new)

        tb, tt = _tiles(bs, ts)
        (qa, kaf, kab, vaf, vab, logf, cum, qb, kb, vb, rb, la, gates) = _in_proj(
            ys, ada_s, wts["norm_mix_pre"], wts, tb=tb, tt=tt, act_dtype=F32)
        pool, page = cache_k.shape[1], cache_k.shape[2]
        cache_kt = cache_k[layer].transpose(0, 2, 3, 1).reshape(pool, FOX_WIDTH, page)
        cache_vt = cache_v[layer].transpose(0, 2, 3, 1).reshape(pool, FOX_WIDTH, page)
        logf_t = cache_logf[layer].transpose(0, 2, 1)
        y_a = _fox_sample(qa.reshape(bs, ts, FOX_WIDTH), cum.reshape(bs, ts, FOX_HEADS),
                          kaf.reshape(bs, ts, FOX_WIDTH), vaf.reshape(bs, ts, FOX_WIDTH),
                          logf.reshape(bs, ts, FOX_HEADS), cache_kt, cache_vt, logf_t, page_table)
        o_b, s_new = _gla(qb, kb, vb, la, rb, wts["gla_norm"], state_gla[layer], bs, ts, ts)
        ys = _tail(ys, ada_s, y_a.astype(BF16), o_b, gates, wts, tb=tb, tt=tt)
        outs[4].append(kaf.reshape(bs, ts, FOX_HEADS, FOX_HEAD_DIM))
        outs[5].append(vaf.reshape(bs, ts, FOX_HEADS, FOX_HEAD_DIM))
        outs[6].append(logf.reshape(bs, ts, FOX_HEADS))
        outs[7].append(s_new)
    return (yp, ys) + tuple(jnp.stack(o) for o in outs)
```
